```python
import jax, jax.numpy as jnp
from jax import lax
import numpy as np

D_MODEL = 1024
BATCH = 8
SEQ = 8192
DEPTH = 1

HEAD_DIM = 128
N_HEADS_GDN = 4
N_HEADS_FOX = 4
W_GDN = N_HEADS_GDN * HEAD_DIM
W_FOX = N_HEADS_FOX * HEAD_DIM
CONV_WIDTH = 4
CHUNK = 64
Q_BLOCK = 128
EPS = 1e-6
SPLITS = (W_GDN, W_GDN, W_GDN, N_HEADS_GDN, N_HEADS_GDN, W_GDN,
          W_FOX, W_FOX, W_FOX, N_HEADS_FOX, W_FOX,
          D_MODEL, D_MODEL)
D_IN = 4 * W_GDN + 2 * N_HEADS_GDN + 4 * W_FOX + N_HEADS_FOX + 2 * D_MODEL

kernel_name = 'hybrid_gdn_fox_gated_merge_block'


def rmsnorm(x, g):
    xf = x.astype(jnp.float32)
    y = xf * lax.rsqrt(jnp.mean(xf * xf, axis=-1, keepdims=True) + EPS)
    return (y * g.astype(jnp.float32)).astype(x.dtype)


def l2norm(x):
    xf = x.astype(jnp.float32)
    return xf * lax.rsqrt(jnp.sum(xf * xf, axis=-1, keepdims=True) + EPS)


def causal_conv_silu(u, w):
    K = w.shape[0]
    T = u.shape[1]
    up = jnp.pad(u, ((0, 0), (K - 1, 0), (0, 0)))
    y = sum(up[:, i:i + T] * w[i] for i in range(K))
    return jax.nn.silu(y)


def gated_delta_rule(q, k, v, g, beta):
    B, T, H, Dk = q.shape
    Dv = v.shape[-1]
    N = T // CHUNK
    f32 = jnp.float32
    q = q.astype(f32) * (Dk ** -0.5)

    def to_chunks(a):
        a = a.astype(f32).reshape((B, N, CHUNK, H) + a.shape[3:])
        return jnp.swapaxes(a, 2, 3)

    qc, kc, vc = to_chunks(q), to_chunks(k), to_chunks(v)
    gc, bc = to_chunks(g), to_chunks(beta)
    g_cum = jnp.cumsum(gc, axis=-1)
    idx = jnp.arange(CHUNK)
    causal = idx[:, None] >= idx[None, :]
    strict = idx[:, None] > idx[None, :]
    diff = g_cum[..., :, None] - g_cum[..., None, :]
    decay = jnp.where(causal, jnp.exp(jnp.where(causal, diff, 0.0)), 0.0)
    k_beta = kc * bc[..., None]
    v_beta = vc * bc[..., None]
    L = jnp.where(strict, jnp.einsum('bnhcd,bnhsd->bnhcs', k_beta, kc) * decay, 0.0)
    eye = jnp.eye(CHUNK, dtype=f32)
    Tm = lax.linalg.triangular_solve(eye + L, jnp.broadcast_to(eye, L.shape), left_side=True, lower=True)
    u = jnp.einsum('bnhcs,bnhse->bnhce', Tm, v_beta)
    w = jnp.einsum('bnhcs,bnhsd->bnhcd', Tm, k_beta * jnp.exp(g_cum)[..., None])
    qk = jnp.einsum('bnhcd,bnhsd->bnhcs', qc, kc) * decay

    def step(S, inp):
        q_i, k_i, u_i, w_i, qk_i, g_i = inp
        v_new = u_i - jnp.einsum('bhcd,bhde->bhce', w_i, S)
        o = jnp.einsum('bhcd,bhde->bhce', q_i * jnp.exp(g_i)[..., None], S) + jnp.einsum('bhcs,bhse->bhce', qk_i, v_new)
        g_last = g_i[..., -1]
        S = S * jnp.exp(g_last)[..., None, None] + jnp.einsum(
            'bhcd,bhce->bhde', k_i * jnp.exp(g_last[..., None] - g_i)[..., None], v_new)
        return S, o

    xs = tuple(jnp.moveaxis(a, 1, 0) for a in (qc, kc, u, w, qk, g_cum))
    S0 = jnp.zeros((B, H, Dk, Dv), f32)
    _, o = lax.scan(step, S0, xs)
    return o.transpose(1, 0, 3, 2, 4).reshape(B, T, H, Dv)


def forgetting_attention(q, k, v, log_f):
    B, T, H, D = q.shape
    nb = T // Q_BLOCK
    cT = jnp.cumsum(log_f, axis=1).transpose(0, 2, 1)
    qb = q.reshape(B, nb, Q_BLOCK, H, D).transpose(1, 0, 2, 3, 4)
    cb = cT.reshape(B, H, nb, Q_BLOCK).transpose(2, 0, 1, 3)
    kpos = jnp.arange(T)
    scale = D ** -0.5

    def block(args):
        q_i, c_i, i = args
        s = jnp.einsum('bqhd,bkhd->bhqk', q_i, k).astype(jnp.float32) * scale
        s = s + c_i[..., :, None] - cT[:, :, None, :]
        qpos = i * Q_BLOCK + jnp.arange(Q_BLOCK)
        s = jnp.where(qpos[:, None] >= kpos[None, :], s, -jnp.inf)
        p = jax.nn.softmax(s, axis=-1)
        return jnp.einsum('bhqk,bkhd->bqhd', p.astype(v.dtype), v)

    o = lax.map(block, (qb, cb, jnp.arange(nb)))
    return o.transpose(1, 0, 2, 3, 4).reshape(B, T, H, D)


def setup_inputs(seed: int = 0) -> dict:
    key = jax.random.key(seed)
    ks = jax.random.split(key, 18)
    D = D_MODEL
    nrm = jax.random.normal
    x = nrm(ks[0], (BATCH, SEQ, D), jnp.float32)
    c = nrm(ks[1], (BATCH, D), jnp.float32)
    w_ada = 0.1 * nrm(ks[2], (D, 3 * D), jnp.float32) * D ** -0.5
    b_ada = 0.02 * nrm(ks[3], (3 * D,), jnp.float32)
    g_norm = 1.0 + 0.05 * nrm(ks[4], (D,), jnp.float32)
    w_in = nrm(ks[5], (D, D_IN), jnp.float32) * D ** -0.5
    conv_w = nrm(ks[6], (CONV_WIDTH, 3 * W_GDN), jnp.float32) * CONV_WIDTH ** -0.5
    A_log = jnp.log(jax.random.uniform(ks[7], (N_HEADS_GDN,), jnp.float32, 1.0, 16.0))
    dt = jnp.exp(jax.random.uniform(ks[8], (N_HEADS_GDN,), jnp.float32, np.log(1e-3), np.log(1e-1)))
    dt_bias = dt + jnp.log(-jnp.expm1(-dt))
    g_gdn_out = 1.0 + 0.05 * nrm(ks[9], (HEAD_DIM,), jnp.float32)
    g_q_fox = 1.0 + 0.05 * nrm(ks[10], (HEAD_DIM,), jnp.float32)
    g_k_fox = 1.0 + 0.05 * nrm(ks[11], (HEAD_DIM,), jnp.float32)
    b_f = jax.random.uniform(ks[12], (N_HEADS_FOX,), jnp.float32, 1.0, 5.0)
    w_o_gdn = nrm(ks[13], (W_GDN, D), jnp.float32) * W_GDN ** -0.5
    w_o_fox = nrm(ks[14], (W_FOX, D), jnp.float32) * W_FOX ** -0.5
    w_out = nrm(ks[15], (D, D), jnp.float32) * D ** -0.5
    return {'x': x, 'c': c, 'w_ada': w_ada, 'b_ada': b_ada, 'g_norm': g_norm, 'w_in': w_in,
            'conv_w': conv_w, 'A_log': A_log, 'dt_bias': dt_bias, 'g_gdn_out': g_gdn_out,
            'g_q_fox': g_q_fox, 'g_k_fox': g_k_fox, 'b_f': b_f, 'w_o_gdn': w_o_gdn,
            'w_o_fox': w_o_fox, 'w_out': w_out}


def reference(x, c, w_ada, b_ada, g_norm, w_in, conv_w, A_log, dt_bias, g_gdn_out,
              g_q_fox, g_k_fox, b_f, w_o_gdn, w_o_fox, w_out):
    B, T, _ = x.shape
    f32 = jnp.float32
    split_idx = np.cumsum(SPLITS)[:-1].tolist()
    for _layer in range(DEPTH):
        mod = (c @ w_ada + b_ada)[:, None, :]
        shift, scale, gate = jnp.split(mod, 3, axis=-1)
        h = rmsnorm(x, g_norm) * (1.0 + scale) + shift
        proj = h @ w_in
        (qa, ka, va, a_a, b_a, za, qf, kf, vf, f_f, zf, ga, gf) = jnp.split(proj, split_idx, axis=-1)

        qkv = causal_conv_silu(jnp.concatenate([qa, ka, va], axis=-1), conv_w)
        qa, ka, va = jnp.split(qkv, 3, axis=-1)
        qa = l2norm(qa.reshape(B, T, N_HEADS_GDN, HEAD_DIM))
        ka = l2norm(ka.reshape(B, T, N_HEADS_GDN, HEAD_DIM))
        va = va.reshape(B, T, N_HEADS_GDN, HEAD_DIM)
        g_dec = -jnp.exp(A_log.astype(f32)) * jax.nn.softplus(a_a.astype(f32) + dt_bias.astype(f32))
        beta = jax.nn.sigmoid(b_a.astype(f32))
        o_a = gated_delta_rule(qa, ka, va, g_dec, beta).astype(x.dtype)
        o_a = rmsnorm(o_a, g_gdn_out) * jax.nn.silu(za.reshape(B, T, N_HEADS_GDN, HEAD_DIM))
        y_a = o_a.reshape(B, T, W_GDN) @ w_o_gdn

        qf = rmsnorm(qf.reshape(B, T, N_HEADS_FOX, HEAD_DIM), g_q_fox)
        kf = rmsnorm(kf.reshape(B, T, N_HEADS_FOX, HEAD_DIM), g_k_fox)
        vf = vf.reshape(B, T, N_HEADS_FOX, HEAD_DIM)
        log_f = jax.nn.log_sigmoid(f_f.astype(f32) + b_f.astype(f32))
        o_f = forgetting_attention(qf, kf, vf, log_f)
        o_f = o_f * jax.nn.silu(zf.reshape(B, T, N_HEADS_FOX, HEAD_DIM))
        y_f = o_f.reshape(B, T, W_FOX) @ w_o_fox

        merged = jax.nn.sigmoid(ga) * y_a + jax.nn.sigmoid(gf) * y_f
        x = x + gate * (merged @ w_out)
    return x
```

```python
import functools

import jax
import jax.numpy as jnp
from jax import lax
from jax.experimental import pallas as pl
from jax.experimental.pallas import tpu as pltpu

F32 = jnp.float32
BF16 = jnp.bfloat16
EPS = 1e-6
HEAD_DIM = 128
N_HEADS = 4
W_MIX = N_HEADS * HEAD_DIM
CONV_WIDTH = 4
GDN_CHUNK = 64
LANES = 128
SUBLANES = 8
VMEM_LIMIT = 56 * 1024 * 1024
HI = lax.Precision.HIGHEST
NEG_INF = float("-inf")

_C_QKVA = 0
_C_ZA = 3 * W_MIX
_C_QF = _C_ZA + W_MIX
_C_KF = _C_QF + W_MIX
_C_VF = _C_KF + W_MIX
_C_ZF = _C_VF + W_MIX
_C_GA = _C_ZF + W_MIX
_L_A, _L_B, _L_F = 0, N_HEADS, 2 * N_HEADS


def _softplus(z):
    return jnp.maximum(z, 0.0) + jnp.log(1.0 + jnp.exp(-jnp.abs(z)))


def _log_sigmoid(z):
    return jnp.minimum(z, 0.0) - jnp.log(1.0 + jnp.exp(-jnp.abs(z)))


def _sigmoid(z):
    return 1.0 / (1.0 + jnp.exp(-z))


def _silu(z):
    return z * _sigmoid(z)


def _split3_bf16(c):
    hi = c.astype(BF16).astype(F32)
    r = c - hi
    mid = r.astype(BF16).astype(F32)
    lo = (r - mid).astype(BF16).astype(F32)
    return hi, mid, lo


def _adaln_kernel(c_ref, w_ref, b_ref, o_ref):
    o_ref[...] = jnp.dot(c_ref[...], w_ref[...], precision=HI, preferred_element_type=F32) + b_ref[...]


def _adaln(c, w_ada, b_ada):
    B, D = c.shape
    n = w_ada.shape[1]
    bn = D
    return pl.pallas_call(
        _adaln_kernel,
        grid=(n // bn,),
        in_specs=[pl.BlockSpec((B, D), lambda j: (0, 0)),
                  pl.BlockSpec((D, bn), lambda j: (0, j)),
                  pl.BlockSpec((1, bn), lambda j: (0, j))],
        out_specs=pl.BlockSpec((B, bn), lambda j: (0, j)),
        out_shape=jax.ShapeDtypeStruct((B, n), F32),
        compiler_params=pltpu.CompilerParams(dimension_semantics=("arbitrary",), vmem_limit_bytes=VMEM_LIMIT),
        name="adaln",
    )(c, w_ada, b_ada.reshape(1, n))


def _inproj_kernel(x_ref, shift_ref, scale_ref, gn_ref, w_ref, gq_ref, gk_ref, bf_ref,
                   qkva_ref, za_ref, qf_ref, kf_ref, vf_ref, zf_ref, ga_ref, gf_ref, gates_ref,
                   carry_ref, *, tm, d_model):
    i = pl.program_id(1)

    @pl.when(i == 0)
    def _():
        carry_ref[...] = jnp.zeros_like(carry_ref)

    x = x_ref[...]
    ms = jnp.mean(x * x, axis=-1, keepdims=True)
    h = x * lax.rsqrt(ms + EPS) * gn_ref[...]
    h = h * (1.0 + scale_ref[...]) + shift_ref[...]
    hb = h.astype(BF16)

    def proj(lo, width):
        return jnp.dot(hb, w_ref[:, lo:lo + width], preferred_element_type=F32)

    for j in range(3):
        qkva_ref[:, j * W_MIX:(j + 1) * W_MIX] = proj(_C_QKVA + j * W_MIX, W_MIX).astype(BF16)
    za_ref[...] = proj(_C_ZA, W_MIX).astype(BF16)
    vf_ref[...] = proj(_C_VF, W_MIX).astype(BF16)
    zf_ref[...] = proj(_C_ZF, W_MIX).astype(BF16)
    for j in range(d_model // W_MIX):
        ga_ref[:, j * W_MIX:(j + 1) * W_MIX] = proj(_C_GA + j * W_MIX, W_MIX).astype(BF16)
        gf_ref[:, j * W_MIX:(j + 1) * W_MIX] = proj(_C_GA + d_model + j * W_MIX, W_MIX).astype(BF16)

    g = proj(_C_GA + 2 * d_model, LANES)
    gates_ref[...] = g
    logf = _log_sigmoid(g + bf_ref[...])
    r = lax.broadcasted_iota(jnp.int32, (tm, tm), 0)
    c = lax.broadcasted_iota(jnp.int32, (tm, tm), 1)
    tri = jnp.where(r >= c, 1.0, 0.0).astype(F32)
    csum = jnp.dot(tri, logf, precision=HI, preferred_element_type=F32) + carry_ref[...]
    carry_ref[...] = csum[tm - 1:tm, :]

    qf = proj(_C_QF, W_MIX)
    kf = proj(_C_KF, W_MIX)
    lane = lax.broadcasted_iota(jnp.int32, (tm, LANES), 1)
    one = jnp.where(lane < 3, 0.0, jnp.where(lane < 6, 1.0, 0.0))
    one_k = jnp.where(lane < 3, 1.0, 0.0)
    for hd in range(N_HEADS):
        sl = slice(hd * HEAD_DIM, (hd + 1) * HEAD_DIM)
        qh = qf[:, sl]
        kh = kf[:, sl]
        qn = qh * lax.rsqrt(jnp.mean(qh * qh, axis=-1, keepdims=True) + EPS) * gq_ref[...]
        kn = kh * lax.rsqrt(jnp.mean(kh * kh, axis=-1, keepdims=True) + EPS) * gk_ref[...]
        ch = jnp.broadcast_to(csum[:, _L_F + hd:_L_F + hd + 1], (tm, LANES))
        hi, mid, lo = _split3_bf16(ch)
        aug_q = jnp.where(lane == 0, hi, jnp.where(lane == 1, mid, jnp.where(lane == 2, lo, one)))
        aug_k = jnp.where(lane == 3, -hi, jnp.where(lane == 4, -mid, jnp.where(lane == 5, -lo, one_k)))
        base = hd * 2 * HEAD_DIM
        qf_ref[:, base:base + HEAD_DIM] = (qn * (HEAD_DIM ** -0.5)).astype(BF16)
        qf_ref[:, base + HEAD_DIM:base + 2 * HEAD_DIM] = aug_q.astype(BF16)
        kf_ref[:, base:base + HEAD_DIM] = kn.astype(BF16)
        kf_ref[:, base + HEAD_DIM:base + 2 * HEAD_DIM] = aug_k.astype(BF16)


def _inproj(x, shift, scale, g_norm, w_perm, g_q, g_k, bf_lane, tm):
    B, T, D = x.shape
    n_w = w_perm.shape[1]
    tok = lambda width: pl.BlockSpec((None, tm, width), lambda b, i: (b, i, 0))
    per_b = pl.BlockSpec((None, 1, D), lambda b, i: (b, 0, 0))
    const = lambda shape: pl.BlockSpec(shape, lambda b, i: (0,) * len(shape))
    out_widths = [3 * W_MIX, W_MIX, 2 * W_MIX, 2 * W_MIX, W_MIX, W_MIX, D, D]
    out_shape = [jax.ShapeDtypeStruct((B, T, w), BF16) for w in out_widths]
    out_shape.append(jax.ShapeDtypeStruct((B, T, LANES), F32))
    return pl.pallas_call(
        functools.partial(_inproj_kernel, tm=tm, d_model=D),
        grid=(B, T // tm),
        in_specs=[tok(D), per_b, per_b, const((1, D)), const((D, n_w)),
                  const((1, HEAD_DIM)), const((1, HEAD_DIM)), const((1, LANES))],
        out_specs=[tok(w) for w in out_widths] + [tok(LANES)],
        out_shape=out_shape,
        scratch_shapes=[pltpu.VMEM((1, LANES), F32)],
        compiler_params=pltpu.CompilerParams(dimension_semantics=("parallel", "arbitrary"),
                                             vmem_limit_bytes=VMEM_LIMIT),
        name="inproj",
    )(x, shift, scale, g_norm, w_perm, g_q, g_k, bf_lane)


def _gdn_kernel(qkv_ref, convw_ref, slab_ref, gt_ref, acol_ref, dcol_ref, arow_ref, drow_ref,
                za_ref, gout_ref, o_ref, xbuf, s_ref, *, tt):
    i = pl.program_id(1)
    C = GDN_CHUNK
    n_chunks = tt // C
    shift = C.bit_length() - 1

    @pl.when(i == 0)
    def _():
        xbuf[0:SUBLANES, :] = jnp.zeros((SUBLANES, 3 * W_MIX), F32)
        s_ref[...] = jnp.zeros_like(s_ref)

    xbuf[SUBLANES:SUBLANES + tt, :] = qkv_ref[...].astype(F32)
    w = convw_ref[...]
    y = xbuf[SUBLANES - 3:SUBLANES - 3 + tt, :] * w[0:1, :]
    for tap in range(1, CONV_WIDTH):
        y = y + xbuf[SUBLANES - 3 + tap:SUBLANES - 3 + tap + tt, :] * w[tap:tap + 1, :]
    xbuf[0:SUBLANES, :] = xbuf[tt:tt + SUBLANES, :]
    y = _silu(y)

    slab = slab_ref[...]
    g_col = -jnp.exp(acol_ref[...]) * _softplus(slab + dcol_ref[...])
    beta_col = _sigmoid(slab)
    g_row = -jnp.exp(arow_ref[...]) * _softplus(gt_ref[...] + drow_ref[...])

    r = lax.broadcasted_iota(jnp.int32, (tt, tt), 0)
    c = lax.broadcasted_iota(jnp.int32, (tt, tt), 1)
    same = jnp.right_shift(r, shift) == jnp.right_shift(c, shift)
    lower = jnp.logical_and(same, r >= c)
    strict = jnp.logical_and(same, r > c)
    upper = jnp.logical_and(same, r <= c)
    f_lower = jnp.where(lower, 1.0, 0.0).astype(F32)
    f_upper = jnp.where(upper, 1.0, 0.0).astype(F32)
    f_same = jnp.where(same, 1.0, 0.0).astype(F32)
    eye = jnp.where(r == c, 1.0, 0.0).astype(F32)
    gc_col = jnp.dot(f_lower, g_col, precision=HI, preferred_element_type=F32)
    gl_col = jnp.dot(f_same, g_col, precision=HI, preferred_element_type=F32)
    gc_row = jnp.dot(g_row, f_upper, precision=HI, preferred_element_type=F32)

    for hd in range(N_HEADS):
        sl = slice(hd * HEAD_DIM, (hd + 1) * HEAD_DIM)
        qh = y[:, sl]
        kh = y[:, W_MIX + hd * HEAD_DIM:W_MIX + (hd + 1) * HEAD_DIM]
        vh = y[:, 2 * W_MIX + hd * HEAD_DIM:2 * W_MIX + (hd + 1) * HEAD_DIM]
        qn = qh * lax.rsqrt(jnp.sum(qh * qh, axis=-1, keepdims=True) + EPS) * (HEAD_DIM ** -0.5)
        kn = kh * lax.rsqrt(jnp.sum(kh * kh, axis=-1, keepdims=True) + EPS)
        gch = gc_col[:, _L_A + hd:_L_A + hd + 1]
        glh = gl_col[:, _L_A + hd:_L_A + hd + 1]
        bh = beta_col[:, _L_B + hd:_L_B + hd + 1]
        e_gc = jnp.exp(gch)
        e_rem = jnp.exp(glh - gch)
        e_tot = jnp.exp(glh)
        kb = kn * bh
        vb = vh * bh
        kbg = (kb * e_gc).astype(BF16)
        qg = (qn * e_gc).astype(BF16)
        kd = (kn * e_rem).astype(BF16)
        knb = kn.astype(BF16)

        diff = gch - gc_row[hd:hd + 1, :]
        decay = jnp.exp(jnp.where(lower, diff, 0.0))
        kq = lax.dot_general(jnp.concatenate([kb, qn], axis=0).astype(BF16), knb,
                             (((1,), (1,)), ((), ())), preferred_element_type=F32)
        a = jnp.where(strict, kq[:tt] * decay, 0.0)
        qk = jnp.where(lower, kq[tt:] * decay, 0.0).astype(BF16)

        ab = a.astype(BF16)
        p = eye - a
        xp = jnp.dot(ab, ab, preferred_element_type=F32)
        n_fac = shift - 1
        for j in range(n_fac):
            xb = xp.astype(BF16)
            p = p + jnp.dot(p.astype(BF16), xb, preferred_element_type=F32)
            if j + 1 < n_fac:
                xp = jnp.dot(xb, xb, preferred_element_type=F32)
        uw = jnp.dot(p.astype(BF16), jnp.concatenate([vb, kb * e_gc], axis=1).astype(BF16),
                     preferred_element_type=F32)
        u = uw[:, :HEAD_DIM]
        wmat = uw[:, HEAD_DIM:].astype(BF16)

        s = s_ref[hd]
        vnew_parts, qs_parts = [], []
        for ci in range(n_chunks):
            rows = slice(ci * C, (ci + 1) * C)
            rr = jnp.dot(jnp.concatenate([wmat[rows], qg[rows]], axis=0), s.astype(BF16),
                         preferred_element_type=F32)
            vnew = u[rows] - rr[:C]
            qs_parts.append(rr[C:])
            vnew_parts.append(vnew)
            upd = lax.dot_general(kd[rows], vnew.astype(BF16), (((0,), (0,)), ((), ())),
                                  preferred_element_type=F32)
            s = s * e_tot[ci * C:ci * C + 1, :] + upd
        s_ref[hd] = s
        vn = jnp.concatenate(vnew_parts, axis=0)
        o = jnp.concatenate(qs_parts, axis=0) + jnp.dot(qk, vn.astype(BF16), preferred_element_type=F32)
        on = o * lax.rsqrt(jnp.mean(o * o, axis=-1, keepdims=True) + EPS) * gout_ref[...]
        o_ref[:, sl] = (on * _silu(za_ref[:, sl].astype(F32))).astype(BF16)


def _gdn(qkva, conv_w, slab, gates_t, a_log, dt_bias, za, g_out, tt):
    B, T, _ = qkva.shape
    tok = lambda width: pl.BlockSpec((None, tt, width), lambda b, i: (b, i, 0))
    const = lambda shape: pl.BlockSpec(shape, lambda b, i: (0,) * len(shape))
    lane_vec = lambda v: jnp.zeros((1, LANES), F32).at[0, _L_A:_L_A + N_HEADS].set(v)
    sub_vec = lambda v: jnp.zeros((SUBLANES, 1), F32).at[0:N_HEADS, 0].set(v)
    return pl.pallas_call(
        functools.partial(_gdn_kernel, tt=tt),
        grid=(B, T // tt),
        in_specs=[tok(3 * W_MIX), const((CONV_WIDTH, 3 * W_MIX)), tok(LANES),
                  pl.BlockSpec((None, SUBLANES, tt), lambda b, i: (b, 0, i)),
                  const((1, LANES)), const((1, LANES)), const((SUBLANES, 1)), const((SUBLANES, 1)),
                  tok(W_MIX), const((1, HEAD_DIM))],
        out_specs=tok(W_MIX),
        out_shape=jax.ShapeDtypeStruct((B, T, W_MIX), BF16),
        scratch_shapes=[pltpu.VMEM((tt + 2 * SUBLANES, 3 * W_MIX), F32),
                        pltpu.VMEM((N_HEADS, HEAD_DIM, HEAD_DIM), F32)],
        compiler_params=pltpu.CompilerParams(dimension_semantics=("parallel", "arbitrary"),
                                             vmem_limit_bytes=VMEM_LIMIT),
        name="gdn",
    )(qkva, conv_w, slab, gates_t, lane_vec(a_log), lane_vec(dt_bias), sub_vec(a_log), sub_vec(dt_bias),
      za, g_out.reshape(1, HEAD_DIM))


def _fox_kernel(q_ref, k_ref, v_ref, zf_ref, o_ref, m_ref, l_ref, acc_ref, *, tq, tk):
    qi = pl.program_id(2)
    q = q_ref[...]
    m_ref[...] = jnp.full_like(m_ref, NEG_INF)
    l_ref[...] = jnp.zeros_like(l_ref)
    acc_ref[...] = jnp.zeros_like(acc_ref)

    def step(k_start, q_row0, masked):
        k = k_ref[pl.ds(k_start, tk), :]
        v = v_ref[pl.ds(k_start, tk), :]
        s = lax.dot_general(q, k, (((1,), (1,)), ((), ())), preferred_element_type=F32)
        if masked:
            row = lax.broadcasted_iota(jnp.int32, (tq, tk), 0) + q_row0
            col = lax.broadcasted_iota(jnp.int32, (tq, tk), 1) + k_start
            s = jnp.where(row >= col, s, NEG_INF)
        m_prev = m_ref[...]
        m_new = jnp.maximum(m_prev, jnp.max(s, axis=-1, keepdims=True))
        alpha = jnp.exp(m_prev - m_new)
        p = jnp.exp(s - m_new)
        l_ref[...] = alpha * l_ref[...] + jnp.sum(p, axis=-1, keepdims=True)
        acc_ref[...] = alpha * acc_ref[...] + jnp.dot(p.astype(BF16), v, preferred_element_type=F32)
        m_ref[...] = m_new

    def off_diag(j, carry):
        step(pl.multiple_of(j * tk, tk), 0, False)
        return carry

    lax.fori_loop(0, qi * (tq // tk), off_diag, 0)
    for d in range(tq // tk):
        step(pl.multiple_of(qi * tq + d * tk, tk), qi * tq, True)

    o = acc_ref[...] / l_ref[...]
    o_ref[...] = (o * _silu(zf_ref[...].astype(F32))).astype(BF16)


def _fox(qf_aug, kf_aug, vf, zf, tq, tk):
    B, T, _ = vf.shape
    return pl.pallas_call(
        functools.partial(_fox_kernel, tq=tq, tk=tk),
        grid=(B, N_HEADS, T // tq),
        in_specs=[pl.BlockSpec((None, tq, 2 * HEAD_DIM), lambda b, h, i: (b, i, h)),
                  pl.BlockSpec((None, T, 2 * HEAD_DIM), lambda b, h, i: (b, 0, h)),
                  pl.BlockSpec((None, T, HEAD_DIM), lambda b, h, i: (b, 0, h)),
                  pl.BlockSpec((None, tq, HEAD_DIM), lambda b, h, i: (b, i, h))],
        out_specs=pl.BlockSpec((None, tq, HEAD_DIM), lambda b, h, i: (b, i, h)),
        out_shape=jax.ShapeDtypeStruct((B, T, W_MIX), BF16),
        scratch_shapes=[pltpu.VMEM((tq, 1), F32), pltpu.VMEM((tq, 1), F32), pltpu.VMEM((tq, HEAD_DIM), F32)],
        compiler_params=pltpu.CompilerParams(dimension_semantics=("parallel", "parallel", "arbitrary"),
                                             vmem_limit_bytes=VMEM_LIMIT),
        name="fox",
    )(qf_aug, kf_aug, vf, zf)


def _outproj_kernel(oa_ref, of_ref, ga_ref, gf_ref, x_ref, gate_ref, wa_ref, wf_ref, wo_ref, o_ref):
    ya = jnp.dot(oa_ref[...], wa_ref[...], preferred_element_type=F32)
    yf = jnp.dot(of_ref[...], wf_ref[...], preferred_element_type=F32)
    merged = _sigmoid(ga_ref[...].astype(F32)) * ya + _sigmoid(gf_ref[...].astype(F32)) * yf
    res = jnp.dot(merged.astype(BF16), wo_ref[...], preferred_element_type=F32)
    o_ref[...] = x_ref[...] + gate_ref[...] * res


def _outproj(oa, of, ga, gf, x, gate, wa, wf, wo, tm):
    B, T, D = x.shape
    tok = lambda width: pl.BlockSpec((None, tm, width), lambda b, i: (b, i, 0))
    const = lambda shape: pl.BlockSpec(shape, lambda b, i: (0,) * len(shape))
    return pl.pallas_call(
        _outproj_kernel,
        grid=(B, T // tm),
        in_specs=[tok(W_MIX), tok(W_MIX), tok(D), tok(D), tok(D),
                  pl.BlockSpec((None, 1, D), lambda b, i: (b, 0, 0)),
                  const((W_MIX, D)), const((W_MIX, D)), const((D, D))],
        out_specs=tok(D),
        out_shape=jax.ShapeDtypeStruct((B, T, D), F32),
        compiler_params=pltpu.CompilerParams(dimension_semantics=("parallel", "parallel"),
                                             vmem_limit_bytes=VMEM_LIMIT),
        name="outproj",
    )(oa, of, ga, gf, x, gate, wa, wf, wo)


def _permute_w_in(w_in, d_model):
    o = 0
    cols = {}
    for name, width in (("qkva", 3 * W_MIX), ("a", N_HEADS), ("b", N_HEADS), ("za", W_MIX),
                        ("qf", W_MIX), ("kf", W_MIX), ("vf", W_MIX), ("f", N_HEADS), ("zf", W_MIX),
                        ("ga", d_model), ("gf", d_model)):
        cols[name] = w_in[:, o:o + width]
        o += width
    gates = jnp.concatenate([cols["a"], cols["b"], cols["f"],
                             jnp.zeros((w_in.shape[0], LANES - 3 * N_HEADS), w_in.dtype)], axis=1)
    return jnp.concatenate([cols["qkva"], cols["za"], cols["qf"], cols["kf"], cols["vf"], cols["zf"],
                            cols["ga"], cols["gf"], gates], axis=1).astype(BF16)


def kernel(x, c, w_ada, b_ada, g_norm, w_in, conv_w, A_log, dt_bias, g_gdn_out,
           g_q_fox, g_k_fox, b_f, w_o_gdn, w_o_fox, w_out):
    B, T, D = x.shape
    tm = min(512, T)
    tt = min(256, T)
    tq = tk = min(512, T)

    mod = _adaln(c, w_ada, b_ada)
    shift, scale, gate = (mod[:, j * D:(j + 1) * D].reshape(B, 1, D) for j in range(3))

    w_perm = _permute_w_in(w_in, D)
    bf_lane = jnp.zeros((1, LANES), F32).at[0, _L_F:_L_F + N_HEADS].set(b_f)
    (qkva, za, qf_aug, kf_aug, vf, zf, ga, gf, slab) = _inproj(
        x, shift, scale, g_norm.reshape(1, D), w_perm,
        g_q_fox.reshape(1, HEAD_DIM), g_k_fox.reshape(1, HEAD_DIM), bf_lane, tm)

    gates_t = jnp.transpose(slab[:, :, :SUBLANES], (0, 2, 1))
    oa = _gdn(qkva, conv_w, slab, gates_t, A_log, dt_bias, za, g_gdn_out, tt)
    of = _fox(qf_aug, kf_aug, vf, zf, tq, tk)
    return _outproj(oa, of, ga, gf, x, gate, w_o_gdn.astype(BF16), w_o_fox.astype(BF16),
                    w_out.astype(BF16), tm)
```

```python
import functools

import jax
import jax.numpy as jnp
from jax import lax
from jax.experimental import pallas as pl
from jax.experimental.pallas import tpu as pltpu

F32 = jnp.float32
BF16 = jnp.bfloat16
EPS = 1e-6
HEAD_DIM = 128
N_HEADS = 4
W_MIX = N_HEADS * HEAD_DIM
CONV_WIDTH = 4
GDN_CHUNK = 64
LANES = 128
SUBLANES = 8
VMEM_LIMIT = 56 * 1024 * 1024
HI = lax.Precision.HIGHEST
NEG_INF = float("-inf")
LOG2E = 1.4426950408889634
FOX_SUM_ROWS = 16

_C_QKVA = 0
_C_ZA = 3 * W_MIX
_C_QF = _C_ZA + W_MIX
_C_KF = _C_QF + W_MIX
_C_ZF = _C_KF + W_MIX
_C_GA = _C_ZF + W_MIX
_L_A, _L_B, _L_F = 0, N_HEADS, 2 * N_HEADS


def _softplus(z):
    return jnp.maximum(z, 0.0) + jnp.log(1.0 + jnp.exp(-jnp.abs(z)))


def _log_sigmoid(z):
    return jnp.minimum(z, 0.0) - jnp.log(1.0 + jnp.exp(-jnp.abs(z)))


def _sigmoid(z):
    return 1.0 / (1.0 + jnp.exp(-z))


def _silu(z):
    return z * _sigmoid(z)


def _split3_bf16(c):
    hi = c.astype(BF16).astype(F32)
    r = c - hi
    mid = r.astype(BF16).astype(F32)
    lo = (r - mid).astype(BF16).astype(F32)
    return hi, mid, lo


def _adaln_kernel(c_ref, w_ref, b_ref, o_ref):
    o_ref[...] = jnp.dot(c_ref[...], w_ref[...], precision=HI, preferred_element_type=F32) + b_ref[...]


def _adaln(c, w_ada, b_ada):
    B, D = c.shape
    n = w_ada.shape[1]
    bn = D
    return pl.pallas_call(
        _adaln_kernel,
        grid=(n // bn,),
        in_specs=[pl.BlockSpec((B, D), lambda j: (0, 0)),
                  pl.BlockSpec((D, bn), lambda j: (0, j)),
                  pl.BlockSpec((1, bn), lambda j: (0, j))],
        out_specs=pl.BlockSpec((B, bn), lambda j: (0, j)),
        out_shape=jax.ShapeDtypeStruct((B, n), F32),
        compiler_params=pltpu.CompilerParams(dimension_semantics=("arbitrary",), vmem_limit_bytes=VMEM_LIMIT),
        name="adaln",
    )(c, w_ada, b_ada.reshape(1, n))


def _inproj_kernel(x_ref, shift_ref, scale_ref, gn_ref, w_ref, wvt_ref, gq_ref, gk_ref, bf_ref,
                   qkva_ref, za_ref, qf_ref, kf_ref, vft_ref, zf_ref, ga_ref, gf_ref, gates_ref,
                   carry_ref, *, tm, d_model):
    i = pl.program_id(1)

    @pl.when(i == 0)
    def _():
        carry_ref[...] = jnp.zeros_like(carry_ref)

    x = x_ref[...]
    ms = jnp.mean(x * x, axis=-1, keepdims=True)
    h = x * lax.rsqrt(ms + EPS) * gn_ref[...]
    h = h * (1.0 + scale_ref[...]) + shift_ref[...]
    hb = h.astype(BF16)

    def proj(lo, width):
        return jnp.dot(hb, w_ref[:, lo:lo + width], preferred_element_type=F32)

    for j in range(3):
        qkva_ref[:, j * W_MIX:(j + 1) * W_MIX] = proj(_C_QKVA + j * W_MIX, W_MIX).astype(BF16)
    za_ref[...] = proj(_C_ZA, W_MIX).astype(BF16)
    vft_ref[...] = lax.dot_general(wvt_ref[...], hb, (((1,), (1,)), ((), ())),
                                   preferred_element_type=F32).astype(BF16)
    zf_ref[...] = proj(_C_ZF, W_MIX).astype(BF16)
    for j in range(d_model // W_MIX):
        ga_ref[:, j * W_MIX:(j + 1) * W_MIX] = proj(_C_GA + j * W_MIX, W_MIX).astype(BF16)
        gf_ref[:, j * W_MIX:(j + 1) * W_MIX] = proj(_C_GA + d_model + j * W_MIX, W_MIX).astype(BF16)

    g = proj(_C_GA + 2 * d_model, LANES)
    gates_ref[...] = g
    logf = _log_sigmoid(g + bf_ref[...])
    r = lax.broadcasted_iota(jnp.int32, (tm, tm), 0)
    c = lax.broadcasted_iota(jnp.int32, (tm, tm), 1)
    tri = jnp.where(r >= c, 1.0, 0.0).astype(F32)
    csum = jnp.dot(tri, logf, precision=HI, preferred_element_type=F32) + carry_ref[...]
    carry_ref[...] = csum[tm - 1:tm, :]

    qf = proj(_C_QF, W_MIX)
    kf = proj(_C_KF, W_MIX)
    lane = lax.broadcasted_iota(jnp.int32, (tm, LANES), 1)
    one = jnp.where(lane < 3, 0.0, jnp.where(lane < 6, 1.0, 0.0))
    one_k = jnp.where(lane < 3, 1.0, 0.0)
    for hd in range(N_HEADS):
        sl = slice(hd * HEAD_DIM, (hd + 1) * HEAD_DIM)
        qh = qf[:, sl]
        kh = kf[:, sl]
        qn = qh * lax.rsqrt(jnp.mean(qh * qh, axis=-1, keepdims=True) + EPS) * gq_ref[...]
        kn = kh * lax.rsqrt(jnp.mean(kh * kh, axis=-1, keepdims=True) + EPS) * gk_ref[...]
        ch = jnp.broadcast_to(csum[:, _L_F + hd:_L_F + hd + 1], (tm, LANES)) * LOG2E
        hi, mid, lo = _split3_bf16(ch)
        aug_q = jnp.where(lane == 0, hi, jnp.where(lane == 1, mid, jnp.where(lane == 2, lo, one)))
        aug_k = jnp.where(lane == 3, -hi, jnp.where(lane == 4, -mid, jnp.where(lane == 5, -lo, one_k)))
        base = hd * 2 * HEAD_DIM
        qf_ref[:, base:base + HEAD_DIM] = (qn * (HEAD_DIM ** -0.5 * LOG2E)).astype(BF16)
        qf_ref[:, base + HEAD_DIM:base + 2 * HEAD_DIM] = aug_q.astype(BF16)
        kf_ref[:, base:base + HEAD_DIM] = kn.astype(BF16)
        kf_ref[:, base + HEAD_DIM:base + 2 * HEAD_DIM] = aug_k.astype(BF16)


def _inproj(x, shift, scale, g_norm, w_perm, w_vt, g_q, g_k, bf_lane, tm):
    B, T, D = x.shape
    n_w = w_perm.shape[1]
    tok = lambda width: pl.BlockSpec((None, tm, width), lambda b, i: (b, i, 0))
    per_b = pl.BlockSpec((None, 1, D), lambda b, i: (b, 0, 0))
    const = lambda shape: pl.BlockSpec(shape, lambda b, i: (0,) * len(shape))
    bf16_out = lambda w: (tok(w), jax.ShapeDtypeStruct((B, T, w), BF16))
    outs = [bf16_out(3 * W_MIX), bf16_out(W_MIX), bf16_out(2 * W_MIX), bf16_out(2 * W_MIX),
            (pl.BlockSpec((None, None, W_MIX, tm), lambda b, i: (b, i, 0, 0)),
             jax.ShapeDtypeStruct((B, T // tm, W_MIX, tm), BF16)),
            bf16_out(W_MIX), bf16_out(D), bf16_out(D),
            (tok(LANES), jax.ShapeDtypeStruct((B, T, LANES), F32))]
    return pl.pallas_call(
        functools.partial(_inproj_kernel, tm=tm, d_model=D),
        grid=(B, T // tm),
        in_specs=[tok(D), per_b, per_b, const((1, D)), const((D, n_w)), const((W_MIX, D)),
                  const((1, HEAD_DIM)), const((1, HEAD_DIM)), const((1, LANES))],
        out_specs=[o[0] for o in outs],
        out_shape=[o[1] for o in outs],
        scratch_shapes=[pltpu.VMEM((1, LANES), F32)],
        compiler_params=pltpu.CompilerParams(dimension_semantics=("parallel", "arbitrary"),
                                             vmem_limit_bytes=VMEM_LIMIT),
        name="inproj",
    )(x, shift, scale, g_norm, w_perm, w_vt, g_q, g_k, bf_lane)


def _gdn_kernel(qkv_ref, convw_ref, slab_ref, gt_ref, acol_ref, dcol_ref, arow_ref, drow_ref,
                za_ref, gout_ref, o_ref, xbuf, s_ref, *, tt):
    i = pl.program_id(1)
    C = GDN_CHUNK
    n_chunks = tt // C
    shift = C.bit_length() - 1

    @pl.when(i == 0)
    def _():
        xbuf[0:SUBLANES, :] = jnp.zeros((SUBLANES, 3 * W_MIX), F32)
        s_ref[...] = jnp.zeros_like(s_ref)

    xbuf[SUBLANES:SUBLANES + tt, :] = qkv_ref[...].astype(F32)
    w = convw_ref[...]
    y = xbuf[SUBLANES - 3:SUBLANES - 3 + tt, :] * w[0:1, :]
    for tap in range(1, CONV_WIDTH):
        y = y + xbuf[SUBLANES - 3 + tap:SUBLANES - 3 + tap + tt, :] * w[tap:tap + 1, :]
    xbuf[0:SUBLANES, :] = xbuf[tt:tt + SUBLANES, :]
    y = _silu(y)

    slab = slab_ref[...]
    g_col = -jnp.exp(acol_ref[...]) * _softplus(slab + dcol_ref[...])
    beta_col = _sigmoid(slab)
    g_row = -jnp.exp(arow_ref[...]) * _softplus(gt_ref[...] + drow_ref[...])

    r = lax.broadcasted_iota(jnp.int32, (tt, tt), 0)
    c = lax.broadcasted_iota(jnp.int32, (tt, tt), 1)
    same = jnp.right_shift(r, shift) == jnp.right_shift(c, shift)
    lower = jnp.logical_and(same, r >= c)
    strict = jnp.logical_and(same, r > c)
    upper = jnp.logical_and(same, r <= c)
    f_lower = jnp.where(lower, 1.0, 0.0).astype(F32)
    f_upper = jnp.where(upper, 1.0, 0.0).astype(F32)
    f_same = jnp.where(same, 1.0, 0.0).astype(F32)
    eye = jnp.where(r == c, 1.0, 0.0).astype(F32)
    gc_col = jnp.dot(f_lower, g_col, precision=HI, preferred_element_type=F32)
    gl_col = jnp.dot(f_same, g_col, precision=HI, preferred_element_type=F32)
    gc_row = jnp.dot(g_row, f_upper, precision=HI, preferred_element_type=F32)

    for hd in range(N_HEADS):
        sl = slice(hd * HEAD_DIM, (hd + 1) * HEAD_DIM)
        qh = y[:, sl]
        kh = y[:, W_MIX + hd * HEAD_DIM:W_MIX + (hd + 1) * HEAD_DIM]
        vh = y[:, 2 * W_MIX + hd * HEAD_DIM:2 * W_MIX + (hd + 1) * HEAD_DIM]
        qn = qh * lax.rsqrt(jnp.sum(qh * qh, axis=-1, keepdims=True) + EPS) * (HEAD_DIM ** -0.5)
        kn = kh * lax.rsqrt(jnp.sum(kh * kh, axis=-1, keepdims=True) + EPS)
        gch = gc_col[:, _L_A + hd:_L_A + hd + 1]
        glh = gl_col[:, _L_A + hd:_L_A + hd + 1]
        bh = beta_col[:, _L_B + hd:_L_B + hd + 1]
        e_gc = jnp.exp(gch)
        e_rem = jnp.exp(glh - gch)
        e_tot = jnp.exp(glh)
        kb = kn * bh
        vb = vh * bh
        kbg = (kb * e_gc).astype(BF16)
        qg = (qn * e_gc).astype(BF16)
        kd = (kn * e_rem).astype(BF16)
        knb = kn.astype(BF16)

        diff = gch - gc_row[hd:hd + 1, :]
        decay = jnp.exp(jnp.where(lower, diff, 0.0))
        kq = lax.dot_general(jnp.concatenate([kb, qn], axis=0).astype(BF16), knb,
                             (((1,), (1,)), ((), ())), preferred_element_type=F32)
        a = jnp.where(strict, kq[:tt] * decay, 0.0)
        qk = jnp.where(lower, kq[tt:] * decay, 0.0).astype(BF16)

        ab = a.astype(BF16)
        p = eye - a
        xp = jnp.dot(ab, ab, preferred_element_type=F32)
        n_fac = shift - 1
        for j in range(n_fac):
            xb = xp.astype(BF16)
            p = p + jnp.dot(p.astype(BF16), xb, preferred_element_type=F32)
            if j + 1 < n_fac:
                xp = jnp.dot(xb, xb, preferred_element_type=F32)
        uw = jnp.dot(p.astype(BF16), jnp.concatenate([vb, kb * e_gc], axis=1).astype(BF16),
                     preferred_element_type=F32)
        u = uw[:, :HEAD_DIM]
        wmat = uw[:, HEAD_DIM:].astype(BF16)

        s = s_ref[hd]
        vnew_parts, qs_parts = [], []
        for ci in range(n_chunks):
            rows = slice(ci * C, (ci + 1) * C)
            rr = jnp.dot(jnp.concatenate([wmat[rows], qg[rows]], axis=0), s.astype(BF16),
                         preferred_element_type=F32)
            vnew = u[rows] - rr[:C]
            qs_parts.append(rr[C:])
            vnew_parts.append(vnew)
            upd = lax.dot_general(kd[rows], vnew.astype(BF16), (((0,), (0,)), ((), ())),
                                  preferred_element_type=F32)
            s = s * e_tot[ci * C:ci * C + 1, :] + upd
        s_ref[hd] = s
        vn = jnp.concatenate(vnew_parts, axis=0)
        o = jnp.concatenate(qs_parts, axis=0) + jnp.dot(qk, vn.astype(BF16), preferred_element_type=F32)
        on = o * lax.rsqrt(jnp.mean(o * o, axis=-1, keepdims=True) + EPS) * gout_ref[...]
        o_ref[:, sl] = (on * _silu(za_ref[:, sl].astype(F32))).astype(BF16)


def _gdn(qkva, conv_w, slab, gates_t, a_log, dt_bias, za, g_out, tt):
    B, T, _ = qkva.shape
    tok = lambda width: pl.BlockSpec((None, tt, width), lambda b, i: (b, i, 0))
    const = lambda shape: pl.BlockSpec(shape, lambda b, i: (0,) * len(shape))
    lane_vec = lambda v: jnp.zeros((1, LANES), F32).at[0, _L_A:_L_A + N_HEADS].set(v)
    sub_vec = lambda v: jnp.zeros((SUBLANES, 1), F32).at[0:N_HEADS, 0].set(v)
    return pl.pallas_call(
        functools.partial(_gdn_kernel, tt=tt),
        grid=(B, T // tt),
        in_specs=[tok(3 * W_MIX), const((CONV_WIDTH, 3 * W_MIX)), tok(LANES),
                  pl.BlockSpec((None, SUBLANES, tt), lambda b, i: (b, 0, i)),
                  const((1, LANES)), const((1, LANES)), const((SUBLANES, 1)), const((SUBLANES, 1)),
                  tok(W_MIX), const((1, HEAD_DIM))],
        out_specs=tok(W_MIX),
        out_shape=jax.ShapeDtypeStruct((B, T, W_MIX), BF16),
        scratch_shapes=[pltpu.VMEM((tt + 2 * SUBLANES, 3 * W_MIX), F32),
                        pltpu.VMEM((N_HEADS, HEAD_DIM, HEAD_DIM), F32)],
        compiler_params=pltpu.CompilerParams(dimension_semantics=("parallel", "arbitrary"),
                                             vmem_limit_bytes=VMEM_LIMIT),
        name="gdn",
    )(qkva, conv_w, slab, gates_t, lane_vec(a_log), lane_vec(dt_bias), sub_vec(a_log), sub_vec(dt_bias),
      za, g_out.reshape(1, HEAD_DIM))


def _fox_kernel(q_ref, k_ref, vt_ref, zf_ref, o_ref, m_ref, acc_ref, sa_ref, sb_ref, *, tq, tk):
    qi = pl.program_id(2)
    m_ref[...] = jnp.full_like(m_ref, NEG_INF)
    acc_ref[...] = jnp.zeros_like(acc_ref)
    ones_rows = jnp.ones((FOX_SUM_ROWS, tk), BF16)
    halves = (slice(0, tk), slice(tk, 2 * tk))

    def scores_into(buf, kb, which=(0, 1)):
        k = k_ref[pl.ds(pl.multiple_of(kb * tk, tk), tk), :]
        for h in which:
            buf[h] = lax.dot_general(k, q_ref[halves[h], :], (((1,), (1,)), ((), ())),
                                     preferred_element_type=F32)

    def consume(buf, kb, modes):
        vt = jnp.concatenate([vt_ref[kb], ones_rows], axis=0)
        live = [h for h in (0, 1) if modes[h] != "skip"]
        pts, alphas = {}, {}
        for h in live:
            st = buf[h]
            if modes[h] == "diag":
                krow = lax.broadcasted_iota(jnp.int32, (tk, tk), 0)
                qcol = lax.broadcasted_iota(jnp.int32, (tk, tk), 1)
                st = jnp.where(qcol >= krow, st, NEG_INF)
            m_prev = m_ref[:, halves[h]]
            m_new = jnp.maximum(m_prev, jnp.max(st, axis=0, keepdims=True))
            alphas[h] = jnp.exp2(m_prev - m_new)
            pts[h] = jnp.exp2(st - m_new).astype(BF16)
            m_ref[:, halves[h]] = m_new
        for h in live:
            acc_ref[:, halves[h]] = (alphas[h] * acc_ref[:, halves[h]]
                                     + jnp.dot(vt, pts[h], preferred_element_type=F32))

    full = ("full", "full")
    scores_into(sa_ref, 0)

    def pair(p, carry):
        kb = 2 * p
        scores_into(sb_ref, kb + 1)
        consume(sa_ref, kb, full)
        scores_into(sa_ref, kb + 2)
        consume(sb_ref, kb + 1, full)
        return carry

    lax.fori_loop(0, qi, pair, 0)
    scores_into(sb_ref, 2 * qi + 1, which=(1,))
    consume(sa_ref, 2 * qi, ("diag", "full"))
    consume(sb_ref, 2 * qi + 1, ("skip", "diag"))

    ot = acc_ref[0:HEAD_DIM, :] * (1.0 / acc_ref[HEAD_DIM:HEAD_DIM + 1, :])
    o_ref[...] = (ot.T * _silu(zf_ref[...].astype(F32))).astype(BF16)


def _fox(qf_aug, kf_aug, vft, zf, tk):
    B, T, _ = zf.shape
    tq = 2 * tk
    assert T % tq == 0
    return pl.pallas_call(
        functools.partial(_fox_kernel, tq=tq, tk=tk),
        grid=(B, N_HEADS, T // tq),
        in_specs=[pl.BlockSpec((None, tq, 2 * HEAD_DIM), lambda b, h, i: (b, i, h)),
                  pl.BlockSpec((None, T, 2 * HEAD_DIM), lambda b, h, i: (b, 0, h)),
                  pl.BlockSpec((None, T // tk, HEAD_DIM, tk), lambda b, h, i: (b, 0, h, 0)),
                  pl.BlockSpec((None, tq, HEAD_DIM), lambda b, h, i: (b, i, h))],
        out_specs=pl.BlockSpec((None, tq, HEAD_DIM), lambda b, h, i: (b, i, h)),
        out_shape=jax.ShapeDtypeStruct((B, T, W_MIX), BF16),
        scratch_shapes=[pltpu.VMEM((1, tq), F32), pltpu.VMEM((HEAD_DIM + FOX_SUM_ROWS, tq), F32),
                        pltpu.VMEM((2, tk, tk), F32), pltpu.VMEM((2, tk, tk), F32)],
        compiler_params=pltpu.CompilerParams(dimension_semantics=("parallel", "parallel", "arbitrary"),
                                             vmem_limit_bytes=VMEM_LIMIT),
        name="fox",
    )(qf_aug, kf_aug, vft, zf)


def _outproj_kernel(oa_ref, of_ref, ga_ref, gf_ref, x_ref, gate_ref, wa_ref, wf_ref, wo_ref, o_ref):
    ya = jnp.dot(oa_ref[...], wa_ref[...], preferred_element_type=F32)
    yf = jnp.dot(of_ref[...], wf_ref[...], preferred_element_type=F32)
    merged = _sigmoid(ga_ref[...].astype(F32)) * ya + _sigmoid(gf_ref[...].astype(F32)) * yf
    res = jnp.dot(merged.astype(BF16), wo_ref[...], preferred_element_type=F32)
    o_ref[...] = x_ref[...] + gate_ref[...] * res


def _outproj(oa, of, ga, gf, x, gate, wa, wf, wo, tm):
    B, T, D = x.shape
    tok = lambda width: pl.BlockSpec((None, tm, width), lambda b, i: (b, i, 0))
    const = lambda shape: pl.BlockSpec(shape, lambda b, i: (0,) * len(shape))
    return pl.pallas_call(
        _outproj_kernel,
        grid=(B, T // tm),
        in_specs=[tok(W_MIX), tok(W_MIX), tok(D), tok(D), tok(D),
                  pl.BlockSpec((None, 1, D), lambda b, i: (b, 0, 0)),
                  const((W_MIX, D)), const((W_MIX, D)), const((D, D))],
        out_specs=tok(D),
        out_shape=jax.ShapeDtypeStruct((B, T, D), F32),
        compiler_params=pltpu.CompilerParams(dimension_semantics=("parallel", "parallel"),
                                             vmem_limit_bytes=VMEM_LIMIT),
        name="outproj",
    )(oa, of, ga, gf, x, gate, wa, wf, wo)


def _permute_w_in(w_in, d_model):
    o = 0
    cols = {}
    for name, width in (("qkva", 3 * W_MIX), ("a", N_HEADS), ("b", N_HEADS), ("za", W_MIX),
                        ("qf", W_MIX), ("kf", W_MIX), ("vf", W_MIX), ("f", N_HEADS), ("zf", W_MIX),
                        ("ga", d_model), ("gf", d_model)):
        cols[name] = w_in[:, o:o + width]
        o += width
    gates = jnp.concatenate([cols["a"], cols["b"], cols["f"],
                             jnp.zeros((w_in.shape[0], LANES - 3 * N_HEADS), w_in.dtype)], axis=1)
    w_perm = jnp.concatenate([cols["qkva"], cols["za"], cols["qf"], cols["kf"], cols["zf"],
                              cols["ga"], cols["gf"], gates], axis=1).astype(BF16)
    return w_perm, cols["vf"].T.astype(BF16)


def kernel(x, c, w_ada, b_ada, g_norm, w_in, conv_w, A_log, dt_bias, g_gdn_out,
           g_q_fox, g_k_fox, b_f, w_o_gdn, w_o_fox, w_out):
    B, T, D = x.shape
    tm = tk = min(512, T)
    tt = min(256, T)

    mod = _adaln(c, w_ada, b_ada)
    shift, scale, gate = (mod[:, j * D:(j + 1) * D].reshape(B, 1, D) for j in range(3))

    w_perm, w_vt = _permute_w_in(w_in, D)
    bf_lane = jnp.zeros((1, LANES), F32).at[0, _L_F:_L_F + N_HEADS].set(b_f)
    (qkva, za, qf_aug, kf_aug, vft, zf, ga, gf, slab) = _inproj(
        x, shift, scale, g_norm.reshape(1, D), w_perm, w_vt,
        g_q_fox.reshape(1, HEAD_DIM), g_k_fox.reshape(1, HEAD_DIM), bf_lane, tm)

    gates_t = jnp.transpose(slab[:, :, :SUBLANES], (0, 2, 1))
    oa = _gdn(qkva, conv_w, slab, gates_t, A_log, dt_bias, za, g_gdn_out, tt)
    of = _fox(qf_aug, kf_aug, vft, zf, tk)
    return _outproj(oa, of, ga, gf, x, gate, w_o_gdn.astype(BF16), w_o_fox.astype(BF16),
                    w_out.astype(BF16), tm)
```

```python
import functools

import jax
import jax.numpy as jnp
from jax import lax
from jax.experimental import pallas as pl
from jax.experimental.pallas import tpu as pltpu

F32 = jnp.float32
BF16 = jnp.bfloat16
EPS = 1e-6
HEAD_DIM = 128
N_HEADS = 4
W_MIX = N_HEADS * HEAD_DIM
CONV_WIDTH = 4
GDN_CHUNK = 64
GDN_BLOCK = 256
MASK_BIG = 1e30
LANES = 128
MXU_TILE = 256
SUBLANES = 8
VMEM_LIMIT = 56 * 1024 * 1024
HI = lax.Precision.HIGHEST
NEG_INF = float("-inf")
LOG2E = 1.4426950408889634
FOX_SUM_ROWS = 16

_C_QKVA = 0
_C_ZA = 3 * W_MIX
_C_QF = _C_ZA + W_MIX
_C_KF = _C_QF + W_MIX
_C_ZF = _C_KF + W_MIX
_C_GA = _C_ZF + W_MIX
_L_A, _L_B, _L_F = 0, N_HEADS, 2 * N_HEADS


def _softplus(z):
    return jnp.maximum(z, 0.0) + jnp.log(1.0 + jnp.exp(-jnp.abs(z)))


def _log_sigmoid(z):
    return jnp.minimum(z, 0.0) - jnp.log(1.0 + jnp.exp(-jnp.abs(z)))


def _sigmoid(z):
    return 1.0 / (1.0 + jnp.exp(-z))


def _silu(z):
    return z * _sigmoid(z)


def _split3_bf16(c):
    hi = c.astype(BF16).astype(F32)
    r = c - hi
    mid = r.astype(BF16).astype(F32)
    lo = (r - mid).astype(BF16).astype(F32)
    return hi, mid, lo


def _adaln_kernel(c_ref, w_ref, b_ref, o_ref):
    o_ref[...] = jnp.dot(c_ref[...], w_ref[...], precision=HI, preferred_element_type=F32) + b_ref[...]


def _adaln(c, w_ada, b_ada):
    B, D = c.shape
    n = w_ada.shape[1]
    bn = D
    return pl.pallas_call(
        _adaln_kernel,
        grid=(n // bn,),
        in_specs=[pl.BlockSpec((B, D), lambda j: (0, 0)),
                  pl.BlockSpec((D, bn), lambda j: (0, j)),
                  pl.BlockSpec((1, bn), lambda j: (0, j))],
        out_specs=pl.BlockSpec((B, bn), lambda j: (0, j)),
        out_shape=jax.ShapeDtypeStruct((B, n), F32),
        compiler_params=pltpu.CompilerParams(dimension_semantics=("arbitrary",), vmem_limit_bytes=VMEM_LIMIT),
        name="adaln",
    )(c, w_ada, b_ada.reshape(1, n))


def _inproj_kernel(x_ref, shift_ref, scale_ref, gn_ref, w_ref, wvt_ref, convw_ref, tri_ref,
                   gq_ref, gk_ref, bf_ref,
                   qkva_ref, za_ref, qf_ref, kf_ref, vft_ref, zf_ref, ga_ref, gf_ref, gates_ref,
                   carry_ref, xbuf, *, tm, d_model):
    i = pl.program_id(1)

    @pl.when(i == 0)
    def _():
        carry_ref[...] = jnp.zeros_like(carry_ref)
        xbuf[0:SUBLANES, :] = jnp.zeros((SUBLANES, 3 * W_MIX), F32)

    x = x_ref[...]
    ms = jnp.mean(x * x, axis=-1, keepdims=True)
    h = x * lax.rsqrt(ms + EPS) * gn_ref[...]
    h = h * (1.0 + scale_ref[...]) + shift_ref[...]
    hb = h.astype(BF16)

    def proj(lo, width):
        return jnp.dot(hb, w_ref[:, lo:lo + width], preferred_element_type=F32)

    NT = MXU_TILE
    RC = tm // 4

    def qkva_tile(t):
        xbuf[SUBLANES:SUBLANES + tm, t * NT:(t + 1) * NT] = proj(_C_QKVA + t * NT, NT)

    def conv_silu(j, rc):
        cols = slice(j * W_MIX, (j + 1) * W_MIX)
        r0 = SUBLANES - 3 + rc * RC
        y = xbuf[r0:r0 + RC, cols] * convw_ref[0:1, cols]
        for tap in range(1, CONV_WIDTH):
            y = y + xbuf[r0 + tap:r0 + tap + RC, cols] * convw_ref[tap:tap + 1, cols]
        qkva_ref[rc * RC:(rc + 1) * RC, cols] = _silu(y).astype(BF16)
        if rc == tm // RC - 1:
            xbuf[0:SUBLANES, cols] = xbuf[tm:tm + SUBLANES, cols]

    def plain(out_ref, lo, t):
        out_ref[:, t * NT:(t + 1) * NT] = proj(lo + t * NT, NT).astype(BF16)

    def vft_tile(t):
        vft_ref[t * NT:(t + 1) * NT, :] = lax.dot_general(
            wvt_ref[t * NT:(t + 1) * NT, :], hb, (((1,), (1,)), ((), ())),
            preferred_element_type=F32).astype(BF16)

    qkva_tile(0)
    qkva_tile(1)
    for t in (2, 3, 4, 5):
        qkva_tile(t)
        conv_silu(0, t - 2)
    g = proj(_C_GA + 2 * d_model, LANES)
    conv_silu(1, 0)
    qk_tiles = {}
    for n, (name, lo) in enumerate((("q", _C_QF), ("q", _C_QF + NT), ("k", _C_KF), ("k", _C_KF + NT))):
        qk_tiles[(name, n % 2)] = proj(lo, NT)
        if n < 3:
            conv_silu(1, n + 1)

    gates_ref[...] = g
    logf = _log_sigmoid(g + bf_ref[...])
    hi, mid, lo = _split3_bf16(logf)
    cs = jnp.dot(tri_ref[...], jnp.concatenate([hi, mid, lo], axis=1).astype(BF16),
                 preferred_element_type=F32)
    csum = (cs[:, :LANES] + cs[:, LANES:2 * LANES]) + cs[:, 2 * LANES:] + carry_ref[...]
    carry_ref[...] = csum[tm - 1:tm, :]

    for n, (ref, lo, t) in enumerate(((za_ref, _C_ZA, 0), (za_ref, _C_ZA, 1),
                                      (zf_ref, _C_ZF, 0), (zf_ref, _C_ZF, 1))):
        plain(ref, lo, t)
        conv_silu(2, n)

    HR = tm // 2
    lane = lax.broadcasted_iota(jnp.int32, (HR, LANES), 1)
    one = jnp.where(lane < 3, 0.0, jnp.where(lane < 6, 1.0, 0.0))
    one_k = jnp.where(lane < 3, 1.0, 0.0)

    def qk_epilogue(hd, half):
        rows = slice(half * HR, (half + 1) * HR)
        sl = slice((hd % 2) * HEAD_DIM, (hd % 2 + 1) * HEAD_DIM)
        qh = qk_tiles[("q", hd // 2)][rows, sl]
        kh = qk_tiles[("k", hd // 2)][rows, sl]
        qn = qh * lax.rsqrt(jnp.mean(qh * qh, axis=-1, keepdims=True) + EPS) * gq_ref[...]
        kn = kh * lax.rsqrt(jnp.mean(kh * kh, axis=-1, keepdims=True) + EPS) * gk_ref[...]
        ch = jnp.broadcast_to(csum[rows, _L_F + hd:_L_F + hd + 1], (HR, LANES)) * LOG2E
        hi, mid, lo = _split3_bf16(ch)
        aug_q = jnp.where(lane == 0, hi, jnp.where(lane == 1, mid, jnp.where(lane == 2, lo, one)))
        aug_k = jnp.where(lane == 3, -hi, jnp.where(lane == 4, -mid, jnp.where(lane == 5, -lo, one_k)))
        base = hd * 2 * HEAD_DIM
        qf_ref[rows, base:base + HEAD_DIM] = (qn * (HEAD_DIM ** -0.5 * LOG2E)).astype(BF16)
        qf_ref[rows, base + HEAD_DIM:base + 2 * HEAD_DIM] = aug_q.astype(BF16)
        kf_ref[rows, base:base + HEAD_DIM] = kn.astype(BF16)
        kf_ref[rows, base + HEAD_DIM:base + 2 * HEAD_DIM] = aug_k.astype(BF16)

    m_steps = ([functools.partial(vft_tile, t) for t in range(W_MIX // NT)]
               + [functools.partial(plain, ga_ref, _C_GA, t) for t in range(d_model // NT)]
               + [functools.partial(plain, gf_ref, _C_GA + d_model, t) for t in range(d_model // NT)])
    v_steps = [functools.partial(qk_epilogue, hd, half) for hd in range(N_HEADS) for half in range(2)]
    for n, m_step in enumerate(m_steps):
        m_step()
        if n < len(v_steps):
            v_steps[n]()


def _inproj(x, shift, scale, g_norm, w_perm, w_vt, conv_w, g_q, g_k, bf_lane, tm):
    B, T, D = x.shape
    n_w = w_perm.shape[1]
    tri = jnp.tril(jnp.ones((tm, tm), BF16))
    tok = lambda width: pl.BlockSpec((None, tm, width), lambda b, i: (b, i, 0))
    per_b = pl.BlockSpec((None, 1, D), lambda b, i: (b, 0, 0))
    const = lambda shape: pl.BlockSpec(shape, lambda b, i: (0,) * len(shape))
    bf16_out = lambda w: (tok(w), jax.ShapeDtypeStruct((B, T, w), BF16))
    outs = [bf16_out(3 * W_MIX), bf16_out(W_MIX), bf16_out(2 * W_MIX), bf16_out(2 * W_MIX),
            (pl.BlockSpec((None, None, W_MIX, tm), lambda b, i: (b, i, 0, 0)),
             jax.ShapeDtypeStruct((B, T // tm, W_MIX, tm), BF16)),
            bf16_out(W_MIX), bf16_out(D), bf16_out(D),
            (tok(LANES), jax.ShapeDtypeStruct((B, T, LANES), F32))]
    return pl.pallas_call(
        functools.partial(_inproj_kernel, tm=tm, d_model=D),
        grid=(B, T // tm),
        in_specs=[tok(D), per_b, per_b, const((1, D)), const((D, n_w)), const((W_MIX, D)),
                  const((CONV_WIDTH, 3 * W_MIX)), const((tm, tm)),
                  const((1, HEAD_DIM)), const((1, HEAD_DIM)), const((1, LANES))],
        out_specs=[o[0] for o in outs],
        out_shape=[o[1] for o in outs],
        scratch_shapes=[pltpu.VMEM((1, LANES), F32), pltpu.VMEM((tm + SUBLANES, 3 * W_MIX), F32)],
        compiler_params=pltpu.CompilerParams(dimension_semantics=("parallel", "arbitrary"),
                                             vmem_limit_bytes=VMEM_LIMIT),
        name="inproj",
    )(x, shift, scale, g_norm, w_perm, w_vt, conv_w, tri, g_q, g_k, bf_lane)


def _gdn_masks():
    r = lax.broadcasted_iota(jnp.int32, (GDN_BLOCK, GDN_BLOCK), 0)
    c = lax.broadcasted_iota(jnp.int32, (GDN_BLOCK, GDN_BLOCK), 1)
    same = (r // GDN_CHUNK) == (c // GDN_CHUNK)
    lower = jnp.logical_and(same, r >= c)
    f_lower = jnp.where(lower, 1.0, 0.0)
    f_same = jnp.where(same, 1.0, 0.0)
    f_cum = jnp.concatenate([f_lower, f_same], axis=0).astype(BF16)
    f_upper = jnp.where(jnp.logical_and(same, r <= c), 1.0, 0.0).astype(BF16)
    eye = jnp.where(r == c, 1.0, 0.0).astype(F32)
    nm_lower = jnp.where(lower, 0.0, -MASK_BIG).astype(F32)
    nm_strict = jnp.where(jnp.logical_and(same, r > c), 0.0, -MASK_BIG).astype(F32)
    return f_cum, f_upper, eye, nm_lower, nm_strict


def _gdn_kernel(qkv_ref, slab_ref, gt_ref, acol_ref, dcol_ref, arow_ref, drow_ref,
                fcum_ref, fupper_ref, eye_ref, nml_ref, nms_ref,
                za_ref, gout_ref, o_ref, s_ref, *, tt):
    i = pl.program_id(1)
    C = GDN_CHUNK
    BLK = GDN_BLOCK
    n_chunks = tt // C
    shift = C.bit_length() - 1

    @pl.when(i == 0)
    def _():
        s_ref[...] = jnp.zeros_like(s_ref)

    y = qkv_ref[...].astype(F32)

    slab = slab_ref[...]
    g_col = -jnp.exp(acol_ref[...]) * _softplus(slab + dcol_ref[...])
    beta_col = _sigmoid(slab)
    g_row = -jnp.exp(arow_ref[...]) * _softplus(gt_ref[...] + drow_ref[...])

    f_cum = fcum_ref[...]
    f_upper = fupper_ref[...]
    eye = eye_ref[...]
    nm_lower = nml_ref[...]
    nm_strict = nms_ref[...]

    def split2(v):
        hi = v.astype(BF16)
        return hi, (v - hi.astype(F32)).astype(BF16)

    probs = [(sb, hd) for sb in range(tt // BLK) for hd in range(N_HEADS)]
    e_gc_s, e_rem_s, e_tot_s, gc_row_s = {}, {}, {}, {}
    for sb in range(tt // BLK):
        rows = slice(sb * BLK, (sb + 1) * BLK)
        g_hi, g_lo = split2(g_col[rows])
        cum = jnp.dot(f_cum, jnp.concatenate([g_hi, g_lo], axis=1), preferred_element_type=F32)
        cum = cum[:, :LANES] + cum[:, LANES:]
        gc_col, gl_col = cum[:BLK], cum[BLK:]
        e_gc_s[sb] = jnp.exp(gc_col)
        e_rem_s[sb] = jnp.exp(gl_col - gc_col)
        e_tot_s[sb] = jnp.exp(gl_col)
        r_hi, r_lo = split2(g_row[:, rows])
        cr = jnp.dot(jnp.concatenate([r_hi, r_lo], axis=0), f_upper, preferred_element_type=F32)
        gc_row_s[sb] = (cr[:SUBLANES] + cr[SUBLANES:], gc_col)

    def col(slab_, lane):
        return jnp.broadcast_to(slab_[:, lane:lane + 1], (slab_.shape[0], HEAD_DIM))

    a_m, qk_m, rhs_uw, qg_m, kd_m, etot_m = {}, {}, {}, {}, {}, {}
    for pr in probs:
        sb, hd = pr
        rows = slice(sb * BLK, (sb + 1) * BLK)
        qh = y[rows, hd * HEAD_DIM:(hd + 1) * HEAD_DIM]
        kh = y[rows, W_MIX + hd * HEAD_DIM:W_MIX + (hd + 1) * HEAD_DIM]
        vh = y[rows, 2 * W_MIX + hd * HEAD_DIM:2 * W_MIX + (hd + 1) * HEAD_DIM]
        qn = qh * lax.rsqrt(jnp.sum(qh * qh, axis=-1, keepdims=True) + EPS) * (HEAD_DIM ** -0.5)
        kn = kh * lax.rsqrt(jnp.sum(kh * kh, axis=-1, keepdims=True) + EPS)
        bh = col(beta_col[rows], _L_B + hd)
        e_gc = col(e_gc_s[sb], _L_A + hd)
        kb = kn * bh
        rhs_uw[pr] = jnp.concatenate([vh * bh, kb * e_gc], axis=1).astype(BF16)
        qg_m[pr] = (qn * e_gc).astype(BF16)
        kd_m[pr] = (kn * col(e_rem_s[sb], _L_A + hd)).astype(BF16)
        etot_m[pr] = col(e_tot_s[sb], _L_A + hd)
        kq = lax.dot_general(jnp.concatenate([kb, qn], axis=0).astype(BF16), kn.astype(BF16),
                             (((1,), (1,)), ((), ())), preferred_element_type=F32)
        gc_row, gc_col = gc_row_s[sb]
        diff = gc_col[:, _L_A + hd:_L_A + hd + 1] - gc_row[hd:hd + 1, :]
        a_m[pr] = kq[:BLK] * jnp.exp(diff + nm_strict)
        qk_m[pr] = (kq[BLK:] * jnp.exp(diff + nm_lower)).astype(BF16)

    n_fac = shift - 1
    p_m, x_m = {}, {}
    for pr in probs:
        ab = a_m[pr].astype(BF16)
        p_m[pr] = eye - a_m[pr]
        x_m[pr] = jnp.dot(ab, ab, preferred_element_type=F32)
    for j in range(n_fac):
        for pr in probs:
            xb = x_m[pr].astype(BF16)
            if j + 1 < n_fac:
                px = jnp.dot(jnp.concatenate([p_m[pr].astype(BF16), xb], axis=0), xb,
                             preferred_element_type=F32)
                p_m[pr] = p_m[pr] + px[:BLK]
                x_m[pr] = px[BLK:]
            else:
                p_m[pr] = p_m[pr] + jnp.dot(p_m[pr].astype(BF16), xb, preferred_element_type=F32)
    u_m, w_m = {}, {}
    for pr in probs:
        uw = jnp.dot(p_m[pr].astype(BF16), rhs_uw[pr], preferred_element_type=F32)
        u_m[pr] = uw[:, :HEAD_DIM]
        w_m[pr] = uw[:, HEAD_DIM:].astype(BF16)

    state = [s_ref[hd] for hd in range(N_HEADS)]
    vnew_parts = {pr: [] for pr in probs}
    qs_parts = {pr: [] for pr in probs}
    for ci in range(n_chunks):
        sb, cb = divmod(ci, BLK // C)
        rows = slice(cb * C, (cb + 1) * C)
        for hd in range(N_HEADS):
            pr = (sb, hd)
            rr = jnp.dot(jnp.concatenate([w_m[pr][rows], qg_m[pr][rows]], axis=0),
                         state[hd].astype(BF16), preferred_element_type=F32)
            vnew = u_m[pr][rows] - rr[:C]
            qs_parts[pr].append(rr[C:])
            vnew_parts[pr].append(vnew)
            upd = lax.dot_general(kd_m[pr][rows], vnew.astype(BF16), (((0,), (0,)), ((), ())),
                                  preferred_element_type=F32)
            state[hd] = state[hd] * etot_m[pr][cb * C:cb * C + 1, :] + upd
    for hd in range(N_HEADS):
        s_ref[hd] = state[hd]

    for pr in probs:
        sb, hd = pr
        rows = slice(sb * BLK, (sb + 1) * BLK)
        sl = slice(hd * HEAD_DIM, (hd + 1) * HEAD_DIM)
        vn = jnp.concatenate(vnew_parts[pr], axis=0).astype(BF16)
        o = jnp.concatenate(qs_parts[pr], axis=0) + jnp.dot(qk_m[pr], vn, preferred_element_type=F32)
        on = o * lax.rsqrt(jnp.mean(o * o, axis=-1, keepdims=True) + EPS) * gout_ref[...]
        o_ref[rows, sl] = (on * _silu(za_ref[rows, sl].astype(F32))).astype(BF16)


def _gdn(qkva, slab, gates_t, a_log, dt_bias, za, g_out, tt):
    B, T, _ = qkva.shape
    tok = lambda width: pl.BlockSpec((None, tt, width), lambda b, i: (b, i, 0))
    const = lambda shape: pl.BlockSpec(shape, lambda b, i: (0,) * len(shape))
    lane_vec = lambda v: jnp.zeros((1, LANES), F32).at[0, _L_A:_L_A + N_HEADS].set(v)
    sub_vec = lambda v: jnp.zeros((SUBLANES, 1), F32).at[0:N_HEADS, 0].set(v)
    masks = _gdn_masks()
    return pl.pallas_call(
        functools.partial(_gdn_kernel, tt=tt),
        grid=(B, T // tt),
        in_specs=[tok(3 * W_MIX), tok(LANES),
                  pl.BlockSpec((None, SUBLANES, tt), lambda b, i: (b, 0, i)),
                  const((1, LANES)), const((1, LANES)), const((SUBLANES, 1)), const((SUBLANES, 1))]
                 + [const(m.shape) for m in masks]
                 + [tok(W_MIX), const((1, HEAD_DIM))],
        out_specs=tok(W_MIX),
        out_shape=jax.ShapeDtypeStruct((B, T, W_MIX), BF16),
        scratch_shapes=[pltpu.VMEM((N_HEADS, HEAD_DIM, HEAD_DIM), F32)],
        compiler_params=pltpu.CompilerParams(dimension_semantics=("parallel", "arbitrary"),
                                             vmem_limit_bytes=VMEM_LIMIT),
        name="gdn",
    )(qkva, slab, gates_t, lane_vec(a_log), lane_vec(dt_bias), sub_vec(a_log), sub_vec(dt_bias),
      *masks, za, g_out.reshape(1, HEAD_DIM))


def _fox_kernel(q_ref, k_ref, vt_ref, zf_ref, o_ref, m_ref, acc_ref, sa_ref, sb_ref, *, tq, tk):
    qi = pl.program_id(2)
    m_ref[...] = jnp.full_like(m_ref, NEG_INF)
    acc_ref[...] = jnp.zeros_like(acc_ref)
    ones_rows = jnp.ones((FOX_SUM_ROWS, tk), BF16)
    halves = (slice(0, tk), slice(tk, 2 * tk))

    def scores_into(buf, kb, which=(0, 1)):
        k = k_ref[pl.ds(pl.multiple_of(kb * tk, tk), tk), :]
        for h in which:
            buf[h] = lax.dot_general(k, q_ref[halves[h], :], (((1,), (1,)), ((), ())),
                                     preferred_element_type=F32)

    def consume(buf, kb, modes):
        vt = jnp.concatenate([vt_ref[kb], ones_rows], axis=0)
        live = [h for h in (0, 1) if modes[h] != "skip"]
        pts, alphas = {}, {}
        for h in live:
            st = buf[h]
            if modes[h] == "diag":
                krow = lax.broadcasted_iota(jnp.int32, (tk, tk), 0)
                qcol = lax.broadcasted_iota(jnp.int32, (tk, tk), 1)
                st = jnp.where(qcol >= krow, st, NEG_INF)
            m_prev = m_ref[:, halves[h]]
            m_new = jnp.maximum(m_prev, jnp.max(st, axis=0, keepdims=True))
            alphas[h] = jnp.exp2(m_prev - m_new)
            pts[h] = jnp.exp2(st - m_new).astype(BF16)
            m_ref[:, halves[h]] = m_new
        for h in live:
            acc_ref[:, halves[h]] = (alphas[h] * acc_ref[:, halves[h]]
                                     + jnp.dot(vt, pts[h], preferred_element_type=F32))

    full = ("full", "full")
    scores_into(sa_ref, 0)

    def pair(p, carry):
        kb = 2 * p
        scores_into(sb_ref, kb + 1)
        consume(sa_ref, kb, full)
        scores_into(sa_ref, kb + 2)
        consume(sb_ref, kb + 1, full)
        return carry

    lax.fori_loop(0, qi, pair, 0)
    scores_into(sb_ref, 2 * qi + 1, which=(1,))
    consume(sa_ref, 2 * qi, ("diag", "full"))
    consume(sb_ref, 2 * qi + 1, ("skip", "diag"))

    ot = acc_ref[0:HEAD_DIM, :] * (1.0 / acc_ref[HEAD_DIM:HEAD_DIM + 1, :])
    o_ref[...] = (ot.T * _silu(zf_ref[...].astype(F32))).astype(BF16)


def _fox(qf_aug, kf_aug, vft, zf, tk):
    B, T, _ = zf.shape
    tq = 2 * tk
    assert T % tq == 0
    return pl.pallas_call(
        functools.partial(_fox_kernel, tq=tq, tk=tk),
        grid=(B, N_HEADS, T // tq),
        in_specs=[pl.BlockSpec((None, tq, 2 * HEAD_DIM), lambda b, h, i: (b, i, h)),
                  pl.BlockSpec((None, T, 2 * HEAD_DIM), lambda b, h, i: (b, 0, h)),
                  pl.BlockSpec((None, T // tk, HEAD_DIM, tk), lambda b, h, i: (b, 0, h, 0)),
                  pl.BlockSpec((None, tq, HEAD_DIM), lambda b, h, i: (b, i, h))],
        out_specs=pl.BlockSpec((None, tq, HEAD_DIM), lambda b, h, i: (b, i, h)),
        out_shape=jax.ShapeDtypeStruct((B, T, W_MIX), BF16),
        scratch_shapes=[pltpu.VMEM((1, tq), F32), pltpu.VMEM((HEAD_DIM + FOX_SUM_ROWS, tq), F32),
                        pltpu.VMEM((2, tk, tk), F32), pltpu.VMEM((2, tk, tk), F32)],
        compiler_params=pltpu.CompilerParams(dimension_semantics=("parallel", "parallel", "arbitrary"),
                                             vmem_limit_bytes=VMEM_LIMIT),
        name="fox",
    )(qf_aug, kf_aug, vft, zf)


def _outproj_kernel(oa_ref, of_ref, ga_ref, gf_ref, x_ref, gate_ref, wa_ref, wf_ref, wo_ref, o_ref):
    ya = jnp.dot(oa_ref[...], wa_ref[...], preferred_element_type=F32)
    yf = jnp.dot(of_ref[...], wf_ref[...], preferred_element_type=F32)
    merged = _sigmoid(ga_ref[...].astype(F32)) * ya + _sigmoid(gf_ref[...].astype(F32)) * yf
    res = jnp.dot(merged.astype(BF16), wo_ref[...], preferred_element_type=F32)
    o_ref[...] = x_ref[...] + gate_ref[...] * res


def _outproj(oa, of, ga, gf, x, gate, wa, wf, wo, tm):
    B, T, D = x.shape
    tok = lambda width: pl.BlockSpec((None, tm, width), lambda b, i: (b, i, 0))
    const = lambda shape: pl.BlockSpec(shape, lambda b, i: (0,) * len(shape))
    return pl.pallas_call(
        _outproj_kernel,
        grid=(B, T // tm),
        in_specs=[tok(W_MIX), tok(W_MIX), tok(D), tok(D), tok(D),
                  pl.BlockSpec((None, 1, D), lambda b, i: (b, 0, 0)),
                  const((W_MIX, D)), const((W_MIX, D)), const((D, D))],
        out_specs=tok(D),
        out_shape=jax.ShapeDtypeStruct((B, T, D), F32),
        compiler_params=pltpu.CompilerParams(dimension_semantics=("parallel", "parallel"),
                                             vmem_limit_bytes=VMEM_LIMIT),
        name="outproj",
    )(oa, of, ga, gf, x, gate, wa, wf, wo)


def _permute_w_in(w_in, d_model):
    o = 0
    cols = {}
    for name, width in (("qkva", 3 * W_MIX), ("a", N_HEADS), ("b", N_HEADS), ("za", W_MIX),
                        ("qf", W_MIX), ("kf", W_MIX), ("vf", W_MIX), ("f", N_HEADS), ("zf", W_MIX),
                        ("ga", d_model), ("gf", d_model)):
        cols[name] = w_in[:, o:o + width]
        o += width
    gates = jnp.concatenate([cols["a"], cols["b"], cols["f"],
                             jnp.zeros((w_in.shape[0], LANES - 3 * N_HEADS), w_in.dtype)], axis=1)
    w_perm = jnp.concatenate([cols["qkva"], cols["za"], cols["qf"], cols["kf"], cols["zf"],
                              cols["ga"], cols["gf"], gates], axis=1).astype(BF16)
    return w_perm, cols["vf"].T.astype(BF16)


def kernel(x, c, w_ada, b_ada, g_norm, w_in, conv_w, A_log, dt_bias, g_gdn_out,
           g_q_fox, g_k_fox, b_f, w_o_gdn, w_o_fox, w_out):
    B, T, D = x.shape
    tm = tk = min(512, T)
    tt = min(256, T)

    mod = _adaln(c, w_ada, b_ada)
    shift, scale, gate = (mod[:, j * D:(j + 1) * D].reshape(B, 1, D) for j in range(3))

    w_perm, w_vt = _permute_w_in(w_in, D)
    bf_lane = jnp.zeros((1, LANES), F32).at[0, _L_F:_L_F + N_HEADS].set(b_f)
    (qkva, za, qf_aug, kf_aug, vft, zf, ga, gf, slab) = _inproj(
        x, shift, scale, g_norm.reshape(1, D), w_perm, w_vt, conv_w,
        g_q_fox.reshape(1, HEAD_DIM), g_k_fox.reshape(1, HEAD_DIM), bf_lane, tm)

    gates_t = jnp.transpose(slab[:, :, :SUBLANES], (0, 2, 1))
    oa = _gdn(qkva, slab, gates_t, A_log, dt_bias, za, g_gdn_out, tt)
    of = _fox(qf_aug, kf_aug, vft, zf, tk)
    return _outproj(oa, of, ga, gf, x, gate, w_o_gdn.astype(BF16), w_o_fox.astype(BF16),
                    w_out.astype(BF16), tm)
```

```python
import functools

import jax
import jax.numpy as jnp
from jax import lax
from jax.experimental import pallas as pl
from jax.experimental.pallas import tpu as pltpu

F32 = jnp.float32
BF16 = jnp.bfloat16
EPS = 1e-6
HEAD_DIM = 128
N_HEADS = 4
W_MIX = N_HEADS * HEAD_DIM
CONV_WIDTH = 4
GDN_CHUNK = 64
GDN_BLOCK = 256
MASK_BIG = 1e30
LANES = 128
MXU_TILE = 256
SUBLANES = 8
VMEM_LIMIT = 56 * 1024 * 1024
HI = lax.Precision.HIGHEST
NEG_INF = float("-inf")
LOG2E = 1.4426950408889634
FOX_SUM_ROWS = 16
FOX_SKIP_LOG2 = 152.0

_C_QKVA = 0
_C_ZA = 3 * W_MIX
_C_QF = _C_ZA + W_MIX
_C_KF = _C_QF + W_MIX
_C_ZF = _C_KF + W_MIX
_C_GA = _C_ZF + W_MIX
_L_A, _L_B, _L_F = 0, N_HEADS, 2 * N_HEADS


def _softplus(z):
    return jnp.maximum(z, 0.0) + jnp.log(1.0 + jnp.exp(-jnp.abs(z)))


def _log_sigmoid(z):
    return jnp.minimum(z, 0.0) - jnp.log(1.0 + jnp.exp(-jnp.abs(z)))


def _sigmoid(z):
    return 1.0 / (1.0 + jnp.exp(-z))


def _silu(z):
    return z * _sigmoid(z)


def _split3_bf16(c):
    hi = c.astype(BF16).astype(F32)
    r = c - hi
    mid = r.astype(BF16).astype(F32)
    lo = (r - mid).astype(BF16).astype(F32)
    return hi, mid, lo


def _adaln_kernel(c_ref, w_ref, b_ref, o_ref):
    o_ref[...] = jnp.dot(c_ref[...], w_ref[...], precision=HI, preferred_element_type=F32) + b_ref[...]


def _adaln(c, w_ada, b_ada):
    B, D = c.shape
    n = w_ada.shape[1]
    bn = D
    return pl.pallas_call(
        _adaln_kernel,
        grid=(n // bn,),
        in_specs=[pl.BlockSpec((B, D), lambda j: (0, 0)),
                  pl.BlockSpec((D, bn), lambda j: (0, j)),
                  pl.BlockSpec((1, bn), lambda j: (0, j))],
        out_specs=pl.BlockSpec((B, bn), lambda j: (0, j)),
        out_shape=jax.ShapeDtypeStruct((B, n), F32),
        compiler_params=pltpu.CompilerParams(dimension_semantics=("arbitrary",), vmem_limit_bytes=VMEM_LIMIT),
        name="adaln",
    )(c, w_ada, b_ada.reshape(1, n))


def _inproj_kernel(x_ref, shift_ref, scale_ref, gn_ref, w_ref, wvt_ref, convw_ref, tri_ref,
                   gq_ref, gk_ref, bf_ref,
                   qkva_ref, za_ref, qf_ref, kf_ref, vft_ref, zf_ref, ga_ref, gf_ref, gates_ref, stats_ref,
                   carry_ref, xbuf, *, tm, d_model):
    i = pl.program_id(1)

    @pl.when(i == 0)
    def _():
        carry_ref[...] = jnp.zeros_like(carry_ref)
        xbuf[0:SUBLANES, :] = jnp.zeros((SUBLANES, 3 * W_MIX), F32)

    x = x_ref[...]
    ms = jnp.mean(x * x, axis=-1, keepdims=True)
    h = x * lax.rsqrt(ms + EPS) * gn_ref[...]
    h = h * (1.0 + scale_ref[...]) + shift_ref[...]
    hb = h.astype(BF16)

    def proj(lo, width):
        return jnp.dot(hb, w_ref[:, lo:lo + width], preferred_element_type=F32)

    NT = MXU_TILE
    RC = tm // 4

    def qkva_tile(t):
        xbuf[SUBLANES:SUBLANES + tm, t * NT:(t + 1) * NT] = proj(_C_QKVA + t * NT, NT)

    def conv_silu(j, rc):
        cols = slice(j * W_MIX, (j + 1) * W_MIX)
        r0 = SUBLANES - 3 + rc * RC
        y = xbuf[r0:r0 + RC, cols] * convw_ref[0:1, cols]
        for tap in range(1, CONV_WIDTH):
            y = y + xbuf[r0 + tap:r0 + tap + RC, cols] * convw_ref[tap:tap + 1, cols]
        qkva_ref[rc * RC:(rc + 1) * RC, cols] = _silu(y).astype(BF16)
        if rc == tm // RC - 1:
            xbuf[0:SUBLANES, cols] = xbuf[tm:tm + SUBLANES, cols]

    def plain(out_ref, lo, t):
        out_ref[:, t * NT:(t + 1) * NT] = proj(lo + t * NT, NT).astype(BF16)

    def vft_tile(t):
        vft_ref[t * NT:(t + 1) * NT, :] = lax.dot_general(
            wvt_ref[t * NT:(t + 1) * NT, :], hb, (((1,), (1,)), ((), ())),
            preferred_element_type=F32).astype(BF16)

    qkva_tile(0)
    qkva_tile(1)
    for t in (2, 3, 4, 5):
        qkva_tile(t)
        conv_silu(0, t - 2)
    g = proj(_C_GA + 2 * d_model, LANES)
    conv_silu(1, 0)
    qk_tiles = {}
    for n, (name, lo) in enumerate((("q", _C_QF), ("q", _C_QF + NT), ("k", _C_KF), ("k", _C_KF + NT))):
        qk_tiles[(name, n % 2)] = proj(lo, NT)
        if n < 3:
            conv_silu(1, n + 1)

    gates_ref[...] = g
    logf = _log_sigmoid(g + bf_ref[...])
    hi, mid, lo = _split3_bf16(logf)
    cs = jnp.dot(tri_ref[...], jnp.concatenate([hi, mid, lo], axis=1).astype(BF16),
                 preferred_element_type=F32)
    csum = (cs[:, :LANES] + cs[:, LANES:2 * LANES]) + cs[:, 2 * LANES:] + carry_ref[...]
    carry_ref[...] = csum[tm - 1:tm, :]

    for n, (ref, lo, t) in enumerate(((za_ref, _C_ZA, 0), (za_ref, _C_ZA, 1),
                                      (zf_ref, _C_ZF, 0), (zf_ref, _C_ZF, 1))):
        plain(ref, lo, t)
        conv_silu(2, n)

    HR = tm // 2
    lane = lax.broadcasted_iota(jnp.int32, (HR, LANES), 1)
    one = jnp.where(lane < 3, 0.0, jnp.where(lane < 6, 1.0, 0.0))
    one_k = jnp.where(lane < 3, 1.0, 0.0)

    def qk_epilogue(hd, half):
        rows = slice(half * HR, (half + 1) * HR)
        sl = slice((hd % 2) * HEAD_DIM, (hd % 2 + 1) * HEAD_DIM)
        qh = qk_tiles[("q", hd // 2)][rows, sl]
        kh = qk_tiles[("k", hd // 2)][rows, sl]
        qn = qh * lax.rsqrt(jnp.mean(qh * qh, axis=-1, keepdims=True) + EPS) * gq_ref[...]
        kn = kh * lax.rsqrt(jnp.mean(kh * kh, axis=-1, keepdims=True) + EPS) * gk_ref[...]
        ch = jnp.broadcast_to(csum[rows, _L_F + hd:_L_F + hd + 1], (HR, LANES)) * LOG2E
        hi, mid, lo = _split3_bf16(ch)
        aug_q = jnp.where(lane == 0, hi, jnp.where(lane == 1, mid, jnp.where(lane == 2, lo, one)))
        aug_k = jnp.where(lane == 3, -hi, jnp.where(lane == 4, -mid, jnp.where(lane == 5, -lo, one_k)))
        base = hd * 2 * HEAD_DIM
        qb = (qn * (HEAD_DIM ** -0.5 * LOG2E)).astype(BF16)
        kb = kn.astype(BF16)
        qf_ref[rows, base:base + HEAD_DIM] = qb
        qf_ref[rows, base + HEAD_DIM:base + 2 * HEAD_DIM] = aug_q.astype(BF16)
        kf_ref[rows, base:base + HEAD_DIM] = kb
        kf_ref[rows, base + HEAD_DIM:base + 2 * HEAD_DIM] = aug_k.astype(BF16)
        for name, val in (("q", qb), ("k", kb)):
            v32 = val.astype(F32)
            sq_max[(name, hd, half)] = jnp.max(jnp.sum(v32 * v32, axis=-1, keepdims=True),
                                               axis=0, keepdims=True)

    sq_max = {}
    m_steps = ([functools.partial(vft_tile, t) for t in range(W_MIX // NT)]
               + [functools.partial(plain, ga_ref, _C_GA, t) for t in range(d_model // NT)]
               + [functools.partial(plain, gf_ref, _C_GA + d_model, t) for t in range(d_model // NT)])
    v_steps = [functools.partial(qk_epilogue, hd, half) for hd in range(N_HEADS) for half in range(2)]
    for n, m_step in enumerate(m_steps):
        m_step()
        if n < len(v_steps):
            v_steps[n]()

    lane1 = lax.broadcasted_iota(jnp.int32, (1, LANES), 1)
    norm_rows = []
    for name in ("q", "k"):
        row = jnp.zeros((1, LANES), F32)
        for hd in range(N_HEADS):
            row = jnp.where(lane1 == hd, jnp.maximum(sq_max[(name, hd, 0)], sq_max[(name, hd, 1)]), row)
        norm_rows.append(row)
    stats_ref[...] = jnp.concatenate(
        [csum[0:1, :] * LOG2E, csum[tm - 1:tm, :] * LOG2E] + norm_rows
        + [jnp.zeros((SUBLANES - 4, LANES), F32)], axis=0)


def _inproj(x, shift, scale, g_norm, w_perm, w_vt, conv_w, g_q, g_k, bf_lane, tm):
    B, T, D = x.shape
    n_w = w_perm.shape[1]
    tri = jnp.tril(jnp.ones((tm, tm), BF16))
    tok = lambda width: pl.BlockSpec((None, tm, width), lambda b, i: (b, i, 0))
    per_b = pl.BlockSpec((None, 1, D), lambda b, i: (b, 0, 0))
    const = lambda shape: pl.BlockSpec(shape, lambda b, i: (0,) * len(shape))
    bf16_out = lambda w: (tok(w), jax.ShapeDtypeStruct((B, T, w), BF16))
    outs = [bf16_out(3 * W_MIX), bf16_out(W_MIX), bf16_out(2 * W_MIX), bf16_out(2 * W_MIX),
            (pl.BlockSpec((None, None, W_MIX, tm), lambda b, i: (b, i, 0, 0)),
             jax.ShapeDtypeStruct((B, T // tm, W_MIX, tm), BF16)),
            bf16_out(W_MIX), bf16_out(D), bf16_out(D),
            (tok(LANES), jax.ShapeDtypeStruct((B, T, LANES), F32)),
            (pl.BlockSpec((None, None, SUBLANES, LANES), lambda b, i: (b, i, 0, 0)),
             jax.ShapeDtypeStruct((B, T // tm, SUBLANES, LANES), F32))]
    return pl.pallas_call(
        functools.partial(_inproj_kernel, tm=tm, d_model=D),
        grid=(B, T // tm),
        in_specs=[tok(D), per_b, per_b, const((1, D)), const((D, n_w)), const((W_MIX, D)),
                  const((CONV_WIDTH, 3 * W_MIX)), const((tm, tm)),
                  const((1, HEAD_DIM)), const((1, HEAD_DIM)), const((1, LANES))],
        out_specs=[o[0] for o in outs],
        out_shape=[o[1] for o in outs],
        scratch_shapes=[pltpu.VMEM((1, LANES), F32), pltpu.VMEM((tm + SUBLANES, 3 * W_MIX), F32)],
        compiler_params=pltpu.CompilerParams(dimension_semantics=("parallel", "arbitrary"),
                                             vmem_limit_bytes=VMEM_LIMIT),
        name="inproj",
    )(x, shift, scale, g_norm, w_perm, w_vt, conv_w, tri, g_q, g_k, bf_lane)


def _gdn_masks():
    r = lax.broadcasted_iota(jnp.int32, (GDN_BLOCK, GDN_BLOCK), 0)
    c = lax.broadcasted_iota(jnp.int32, (GDN_BLOCK, GDN_BLOCK), 1)
    same = (r // GDN_CHUNK) == (c // GDN_CHUNK)
    lower = jnp.logical_and(same, r >= c)
    f_lower = jnp.where(lower, 1.0, 0.0)
    f_same = jnp.where(same, 1.0, 0.0)
    f_cum = jnp.concatenate([f_lower, f_same], axis=0).astype(BF16)
    f_upper = jnp.where(jnp.logical_and(same, r <= c), 1.0, 0.0).astype(BF16)
    eye = jnp.where(r == c, 1.0, 0.0).astype(F32)
    nm_lower = jnp.where(lower, 0.0, -MASK_BIG).astype(F32)
    nm_strict = jnp.where(jnp.logical_and(same, r > c), 0.0, -MASK_BIG).astype(F32)
    return f_cum, f_upper, eye, nm_lower, nm_strict


def _gdn_kernel(qkv_ref, slab_ref, gt_ref, acol_ref, dcol_ref, arow_ref, drow_ref,
                fcum_ref, fupper_ref, eye_ref, nml_ref, nms_ref,
                za_ref, gout_ref, o_ref, s_ref, *, tt):
    i = pl.program_id(1)
    C = GDN_CHUNK
    BLK = GDN_BLOCK
    n_chunks = tt // C
    shift = C.bit_length() - 1

    @pl.when(i == 0)
    def _():
        s_ref[...] = jnp.zeros_like(s_ref)

    f_cum = fcum_ref[...]
    f_upper = fupper_ref[...]
    eye = eye_ref[...]
    nm_lower = nml_ref[...]
    nm_strict = nms_ref[...]
    n_blk = tt // BLK
    per_blk = BLK // C
    n_fac = shift - 1
    assert per_blk == N_HEADS

    def split2(v):
        hi = v.astype(BF16)
        return hi, (v - hi.astype(F32)).astype(BF16)

    def col(slab_, lane):
        return jnp.broadcast_to(slab_[:, lane:lane + 1], (slab_.shape[0], HEAD_DIM))

    gates = {}

    def prep_gates(sb):
        rows = slice(sb * BLK, (sb + 1) * BLK)
        slab = slab_ref[rows, :]
        g_col = -jnp.exp(acol_ref[...]) * _softplus(slab + dcol_ref[...])
        g_row = -jnp.exp(arow_ref[...]) * _softplus(gt_ref[:, rows] + drow_ref[...])
        g_hi, g_lo = split2(g_col)
        cum = jnp.dot(f_cum, jnp.concatenate([g_hi, g_lo], axis=1), preferred_element_type=F32)
        cum = cum[:, :LANES] + cum[:, LANES:]
        gc_col, gl_col = cum[:BLK], cum[BLK:]
        r_hi, r_lo = split2(g_row)
        cr = jnp.dot(jnp.concatenate([r_hi, r_lo], axis=0), f_upper, preferred_element_type=F32)
        gates[sb] = dict(beta=_sigmoid(slab), gc_col=gc_col, gc_row=cr[:SUBLANES] + cr[SUBLANES:],
                         e_gc=jnp.exp(gc_col), e_rem=jnp.exp(gl_col - gc_col), e_tot=jnp.exp(gl_col))

    qk_m, rhs_uw, qg_m, kd_m, etot_m, p_m, x_m, u_m, w_m = {}, {}, {}, {}, {}, {}, {}, {}, {}

    def prep_head(sb, hd):
        pr = (sb, hd)
        gt = gates[sb]
        rows = slice(sb * BLK, (sb + 1) * BLK)
        qh = qkv_ref[rows, hd * HEAD_DIM:(hd + 1) * HEAD_DIM].astype(F32)
        kh = qkv_ref[rows, W_MIX + hd * HEAD_DIM:W_MIX + (hd + 1) * HEAD_DIM].astype(F32)
        vh = qkv_ref[rows, 2 * W_MIX + hd * HEAD_DIM:2 * W_MIX + (hd + 1) * HEAD_DIM].astype(F32)
        qn = qh * lax.rsqrt(jnp.sum(qh * qh, axis=-1, keepdims=True) + EPS) * (HEAD_DIM ** -0.5)
        kn = kh * lax.rsqrt(jnp.sum(kh * kh, axis=-1, keepdims=True) + EPS)
        bh = col(gt["beta"], _L_B + hd)
        e_gc = col(gt["e_gc"], _L_A + hd)
        kb = kn * bh
        rhs_uw[pr] = jnp.concatenate([vh * bh, kb * e_gc], axis=1).astype(BF16)
        qg_m[pr] = (qn * e_gc).astype(BF16)
        kd_m[pr] = (kn * col(gt["e_rem"], _L_A + hd)).astype(BF16)
        etot_m[pr] = col(gt["e_tot"], _L_A + hd)
        kq = lax.dot_general(jnp.concatenate([kb, qn], axis=0).astype(BF16), kn.astype(BF16),
                             (((1,), (1,)), ((), ())), preferred_element_type=F32)
        diff = gt["gc_col"][:, _L_A + hd:_L_A + hd + 1] - gt["gc_row"][hd:hd + 1, :]
        a = kq[:BLK] * jnp.exp(diff + nm_strict)
        qk_m[pr] = (kq[BLK:] * jnp.exp(diff + nm_lower)).astype(BF16)
        ab = a.astype(BF16)
        p_m[pr] = eye - a
        x_m[pr] = jnp.dot(ab, ab, preferred_element_type=F32)

    def inv_iter(sb, j):
        for hd in range(N_HEADS):
            pr = (sb, hd)
            xb = x_m[pr].astype(BF16)
            if j + 1 < n_fac:
                px = jnp.dot(jnp.concatenate([p_m[pr].astype(BF16), xb], axis=0), xb,
                             preferred_element_type=F32)
                p_m[pr] = p_m[pr] + px[:BLK]
                x_m[pr] = px[BLK:]
            else:
                p_m[pr] = p_m[pr] + jnp.dot(p_m[pr].astype(BF16), xb, preferred_element_type=F32)

    def apply_inverse(sb):
        for hd in range(N_HEADS):
            pr = (sb, hd)
            uw = jnp.dot(p_m[pr].astype(BF16), rhs_uw[pr], preferred_element_type=F32)
            u_m[pr] = uw[:, :HEAD_DIM]
            w_m[pr] = uw[:, HEAD_DIM:].astype(BF16)

    state = [s_ref[hd] for hd in range(N_HEADS)]
    vnew_parts, qs_parts = {}, {}

    def state_chunk(sb, cb):
        rows = slice(cb * C, (cb + 1) * C)
        for hd in range(N_HEADS):
            pr = (sb, hd)
            rr = jnp.dot(jnp.concatenate([w_m[pr][rows], qg_m[pr][rows]], axis=0),
                         state[hd].astype(BF16), preferred_element_type=F32)
            vnew = u_m[pr][rows] - rr[:C]
            qs_parts.setdefault(pr, []).append(rr[C:])
            vnew_parts.setdefault(pr, []).append(vnew)
            upd = lax.dot_general(kd_m[pr][rows], vnew.astype(BF16), (((0,), (0,)), ((), ())),
                                  preferred_element_type=F32)
            state[hd] = state[hd] * etot_m[pr][cb * C:cb * C + 1, :] + upd

    def out_head(sb, hd):
        pr = (sb, hd)
        rows = slice(sb * BLK, (sb + 1) * BLK)
        sl = slice(hd * HEAD_DIM, (hd + 1) * HEAD_DIM)
        vn = jnp.concatenate(vnew_parts[pr], axis=0).astype(BF16)
        o = jnp.concatenate(qs_parts[pr], axis=0) + jnp.dot(qk_m[pr], vn, preferred_element_type=F32)
        on = o * lax.rsqrt(jnp.mean(o * o, axis=-1, keepdims=True) + EPS) * gout_ref[...]
        o_ref[rows, sl] = (on * _silu(za_ref[rows, sl].astype(F32))).astype(BF16)

    inv_share = [[] for _ in range(per_blk)]
    for j in range(n_fac):
        inv_share[max(0, j - (n_fac - per_blk))].append(j)
    for t in range(-2, n_blk + 1):
        if 0 <= t + 2 < n_blk:
            prep_gates(t + 2)
        for slot in range(per_blk):
            if 0 <= t < n_blk:
                state_chunk(t, slot)
            if 0 <= t + 1 < n_blk:
                for j in inv_share[slot]:
                    inv_iter(t + 1, j)
                if slot == per_blk - 1:
                    apply_inverse(t + 1)
            if 0 <= t + 2 < n_blk:
                prep_head(t + 2, slot)
            if 0 <= t - 1 < n_blk:
                out_head(t - 1, slot)
    for hd in range(N_HEADS):
        s_ref[hd] = state[hd]


def _gdn(qkva, slab, gates_t, a_log, dt_bias, za, g_out, tt):
    B, T, _ = qkva.shape
    tok = lambda width: pl.BlockSpec((None, tt, width), lambda b, i: (b, i, 0))
    const = lambda shape: pl.BlockSpec(shape, lambda b, i: (0,) * len(shape))
    lane_vec = lambda v: jnp.zeros((1, LANES), F32).at[0, _L_A:_L_A + N_HEADS].set(v)
    sub_vec = lambda v: jnp.zeros((SUBLANES, 1), F32).at[0:N_HEADS, 0].set(v)
    masks = _gdn_masks()
    return pl.pallas_call(
        functools.partial(_gdn_kernel, tt=tt),
        grid=(B, T // tt),
        in_specs=[tok(3 * W_MIX), tok(LANES),
                  pl.BlockSpec((None, SUBLANES, tt), lambda b, i: (b, 0, i)),
                  const((1, LANES)), const((1, LANES)), const((SUBLANES, 1)), const((SUBLANES, 1))]
                 + [const(m.shape) for m in masks]
                 + [tok(W_MIX), const((1, HEAD_DIM))],
        out_specs=tok(W_MIX),
        out_shape=jax.ShapeDtypeStruct((B, T, W_MIX), BF16),
        scratch_shapes=[pltpu.VMEM((N_HEADS, HEAD_DIM, HEAD_DIM), F32)],
        compiler_params=pltpu.CompilerParams(dimension_semantics=("parallel", "arbitrary"),
                                             vmem_limit_bytes=VMEM_LIMIT),
        name="gdn",
    )(qkva, slab, gates_t, lane_vec(a_log), lane_vec(dt_bias), sub_vec(a_log), sub_vec(dt_bias),
      *masks, za, g_out.reshape(1, HEAD_DIM))


def _fox_kernel(fp_ref, q_ref, k_ref, vt_ref, zf_ref, o_ref, m_ref, acc_ref, sa_ref, sb_ref, *, tq, tk, n_q):
    qi = pl.program_id(2)
    p0 = fp_ref[(pl.program_id(0) * N_HEADS + pl.program_id(1)) * n_q + qi]
    m_ref[...] = jnp.full_like(m_ref, NEG_INF)
    acc_ref[...] = jnp.zeros_like(acc_ref)
    ones_rows = jnp.ones((FOX_SUM_ROWS, tk), BF16)
    halves = (slice(0, tk), slice(tk, 2 * tk))

    def scores_into(buf, kb, which=(0, 1)):
        k = k_ref[pl.ds(pl.multiple_of(kb * tk, tk), tk), :]
        for h in which:
            buf[h] = lax.dot_general(k, q_ref[halves[h], :], (((1,), (1,)), ((), ())),
                                     preferred_element_type=F32)

    def consume(buf, kb, modes):
        vt = jnp.concatenate([vt_ref[kb], ones_rows], axis=0)
        live = [h for h in (0, 1) if modes[h] != "skip"]
        pts, alphas = {}, {}
        for h in live:
            st = buf[h]
            if modes[h] == "diag":
                krow = lax.broadcasted_iota(jnp.int32, (tk, tk), 0)
                qcol = lax.broadcasted_iota(jnp.int32, (tk, tk), 1)
                st = jnp.where(qcol >= krow, st, NEG_INF)
            m_prev = m_ref[:, halves[h]]
            m_new = jnp.maximum(m_prev, jnp.max(st, axis=0, keepdims=True))
            alphas[h] = jnp.exp2(m_prev - m_new)
            pts[h] = jnp.exp2(st - m_new).astype(BF16)
            m_ref[:, halves[h]] = m_new
        for h in live:
            acc_ref[:, halves[h]] = (alphas[h] * acc_ref[:, halves[h]]
                                     + jnp.dot(vt, pts[h], preferred_element_type=F32))

    full = ("full", "full")
    scores_into(sa_ref, 2 * p0)

    def pair(p, carry):
        kb = 2 * p
        scores_into(sb_ref, kb + 1)
        consume(sa_ref, kb, full)
        scores_into(sa_ref, kb + 2)
        consume(sb_ref, kb + 1, full)
        return carry

    lax.fori_loop(p0, qi, pair, 0)
    scores_into(sb_ref, 2 * qi + 1, which=(1,))
    consume(sa_ref, 2 * qi, ("diag", "full"))
    consume(sb_ref, 2 * qi + 1, ("skip", "diag"))

    ot = acc_ref[0:HEAD_DIM, :] * (1.0 / acc_ref[HEAD_DIM:HEAD_DIM + 1, :])
    o_ref[...] = (ot.T * _silu(zf_ref[...].astype(F32))).astype(BF16)


def _fox_first_pair(stats, tk, n_q):
    c_first = stats[:, :, 0, _L_F:_L_F + N_HEADS]
    c_last = stats[:, :, 1, _L_F:_L_F + N_HEADS]
    qk_bound = jnp.sqrt(jnp.max(stats[:, :, 2, :N_HEADS], axis=1) * jnp.max(stats[:, :, 3, :N_HEADS], axis=1))
    n_kb = stats.shape[1]
    upper = (2.0 * qk_bound[:, None, None, :] + c_first[:, 0::2, None, :][:, :n_q]
             - c_last[:, None, :, :])
    kb = jnp.arange(n_kb, dtype=jnp.int32)[None, None, :, None]
    diag = 2 * jnp.arange(n_q, dtype=jnp.int32)[None, :, None, None]
    needed = jnp.logical_or(upper >= -FOX_SKIP_LOG2, kb >= diag)
    first_block = jnp.min(jnp.where(needed, kb, n_kb), axis=2)
    return jnp.right_shift(first_block, 1).transpose(0, 2, 1).reshape(-1)


def _fox(qf_aug, kf_aug, vft, zf, stats, tk):
    B, T, _ = zf.shape
    tq = 2 * tk
    assert T % tq == 0
    n_q = T // tq
    first_pair = _fox_first_pair(stats, tk, n_q)
    grid_spec = pltpu.PrefetchScalarGridSpec(
        num_scalar_prefetch=1,
        grid=(B, N_HEADS, n_q),
        in_specs=[pl.BlockSpec((None, tq, 2 * HEAD_DIM), lambda b, h, i, fp: (b, i, h)),
                  pl.BlockSpec((None, T, 2 * HEAD_DIM), lambda b, h, i, fp: (b, 0, h)),
                  pl.BlockSpec((None, T // tk, HEAD_DIM, tk), lambda b, h, i, fp: (b, 0, h, 0)),
                  pl.BlockSpec((None, tq, HEAD_DIM), lambda b, h, i, fp: (b, i, h))],
        out_specs=pl.BlockSpec((None, tq, HEAD_DIM), lambda b, h, i, fp: (b, i, h)),
        scratch_shapes=[pltpu.VMEM((1, tq), F32), pltpu.VMEM((HEAD_DIM + FOX_SUM_ROWS, tq), F32),
                        pltpu.VMEM((2, tk, tk), F32), pltpu.VMEM((2, tk, tk), F32)])
    return pl.pallas_call(
        functools.partial(_fox_kernel, tq=tq, tk=tk, n_q=n_q),
        grid_spec=grid_spec,
        out_shape=jax.ShapeDtypeStruct((B, T, W_MIX), BF16),
        compiler_params=pltpu.CompilerParams(dimension_semantics=("parallel", "parallel", "arbitrary"),
                                             vmem_limit_bytes=VMEM_LIMIT),
        name="fox",
    )(first_pair, qf_aug, kf_aug, vft, zf)


def _outproj_kernel(oa_ref, of_ref, ga_ref, gf_ref, x_ref, gate_ref, wa_ref, wf_ref, wo_ref, o_ref):
    ya = jnp.dot(oa_ref[...], wa_ref[...], preferred_element_type=F32)
    yf = jnp.dot(of_ref[...], wf_ref[...], preferred_element_type=F32)
    merged = _sigmoid(ga_ref[...].astype(F32)) * ya + _sigmoid(gf_ref[...].astype(F32)) * yf
    res = jnp.dot(merged.astype(BF16), wo_ref[...], preferred_element_type=F32)
    o_ref[...] = x_ref[...] + gate_ref[...] * res


def _outproj(oa, of, ga, gf, x, gate, wa, wf, wo, tm):
    B, T, D = x.shape
    tok = lambda width: pl.BlockSpec((None, tm, width), lambda b, i: (b, i, 0))
    const = lambda shape: pl.BlockSpec(shape, lambda b, i: (0,) * len(shape))
    return pl.pallas_call(
        _outproj_kernel,
        grid=(B, T // tm),
        in_specs=[tok(W_MIX), tok(W_MIX), tok(D), tok(D), tok(D),
                  pl.BlockSpec((None, 1, D), lambda b, i: (b, 0, 0)),
                  const((W_MIX, D)), const((W_MIX, D)), const((D, D))],
        out_specs=tok(D),
        out_shape=jax.ShapeDtypeStruct((B, T, D), F32),
        compiler_params=pltpu.CompilerParams(dimension_semantics=("parallel", "parallel"),
                                             vmem_limit_bytes=VMEM_LIMIT),
        name="outproj",
    )(oa, of, ga, gf, x, gate, wa, wf, wo)


def _permute_w_in(w_in, d_model):
    o = 0
    cols = {}
    for name, width in (("qkva", 3 * W_MIX), ("a", N_HEADS), ("b", N_HEADS), ("za", W_MIX),
                        ("qf", W_MIX), ("kf", W_MIX), ("vf", W_MIX), ("f", N_HEADS), ("zf", W_MIX),
                        ("ga", d_model), ("gf", d_model)):
        cols[name] = w_in[:, o:o + width]
        o += width
    gates = jnp.concatenate([cols["a"], cols["b"], cols["f"],
                             jnp.zeros((w_in.shape[0], LANES - 3 * N_HEADS), w_in.dtype)], axis=1)
    w_perm = jnp.concatenate([cols["qkva"], cols["za"], cols["qf"], cols["kf"], cols["zf"],
                              cols["ga"], cols["gf"], gates], axis=1).astype(BF16)
    return w_perm, cols["vf"].T.astype(BF16)


def kernel(x, c, w_ada, b_ada, g_norm, w_in, conv_w, A_log, dt_bias, g_gdn_out,
           g_q_fox, g_k_fox, b_f, w_o_gdn, w_o_fox, w_out):
    B, T, D = x.shape
    tm = tk = min(512, T)
    tt = min(4 * GDN_BLOCK, T)

    mod = _adaln(c, w_ada, b_ada)
    shift, scale, gate = (mod[:, j * D:(j + 1) * D].reshape(B, 1, D) for j in range(3))

    w_perm, w_vt = _permute_w_in(w_in, D)
    bf_lane = jnp.zeros((1, LANES), F32).at[0, _L_F:_L_F + N_HEADS].set(b_f)
    (qkva, za, qf_aug, kf_aug, vft, zf, ga, gf, slab, stats) = _inproj(
        x, shift, scale, g_norm.reshape(1, D), w_perm, w_vt, conv_w,
        g_q_fox.reshape(1, HEAD_DIM), g_k_fox.reshape(1, HEAD_DIM), bf_lane, tm)

    gates_t = jnp.transpose(slab[:, :, :SUBLANES], (0, 2, 1))
    oa = _gdn(qkva, slab, gates_t, A_log, dt_bias, za, g_gdn_out, tt)
    of = _fox(qf_aug, kf_aug, vft, zf, stats, tk)
    return _outproj(oa, of, ga, gf, x, gate, w_o_gdn.astype(BF16), w_o_fox.astype(BF16),
                    w_out.astype(BF16), tm)
```

```python
import functools

import jax
import jax.numpy as jnp
from jax import lax
from jax.experimental import pallas as pl
from jax.experimental.pallas import tpu as pltpu

F32 = jnp.float32
BF16 = jnp.bfloat16
EPS = 1e-6
HEAD_DIM = 128
N_HEADS = 4
W_MIX = N_HEADS * HEAD_DIM
CONV_WIDTH = 4
GDN_CHUNK = 64
GDN_BLOCK = 256
MASK_BIG = 1e30
LANES = 128
MXU_TILE = 256
SUBLANES = 8
VMEM_LIMIT = 56 * 1024 * 1024
HI = lax.Precision.HIGHEST
NEG_INF = float("-inf")
LOG2E = 1.4426950408889634
FOX_SUM_ROWS = 16
FOX_SKIP_LOG2 = 152.0

_C_QKVA = 0
_C_ZA = 3 * W_MIX
_C_QF = _C_ZA + W_MIX
_C_KF = _C_QF + W_MIX
_C_VF = _C_KF + W_MIX
_C_ZF = _C_VF + W_MIX
_C_GA = _C_ZF + W_MIX
_L_A, _L_B, _L_F = 0, N_HEADS, 2 * N_HEADS


def _softplus(z):
    return jnp.maximum(z, 0.0) + jnp.log(1.0 + jnp.exp(-jnp.abs(z)))


def _log_sigmoid(z):
    return jnp.minimum(z, 0.0) - jnp.log(1.0 + jnp.exp(-jnp.abs(z)))


def _sigmoid(z):
    return 1.0 / (1.0 + jnp.exp(-z))


def _silu(z):
    return z * _sigmoid(z)


def _split3_bf16(c):
    hi = c.astype(BF16).astype(F32)
    r = c - hi
    mid = r.astype(BF16).astype(F32)
    lo = (r - mid).astype(BF16).astype(F32)
    return hi, mid, lo


def _adaln_kernel(c_ref, w_ref, b_ref, o_ref):
    o_ref[...] = jnp.dot(c_ref[...], w_ref[...], precision=HI, preferred_element_type=F32) + b_ref[...]


def _adaln(c, w_ada, b_ada):
    B, D = c.shape
    n = w_ada.shape[1]
    bn = D
    return pl.pallas_call(
        _adaln_kernel,
        grid=(n // bn,),
        in_specs=[pl.BlockSpec((B, D), lambda j: (0, 0)),
                  pl.BlockSpec((D, bn), lambda j: (0, j)),
                  pl.BlockSpec((1, bn), lambda j: (0, j))],
        out_specs=pl.BlockSpec((B, bn), lambda j: (0, j)),
        out_shape=jax.ShapeDtypeStruct((B, n), F32),
        compiler_params=pltpu.CompilerParams(dimension_semantics=("arbitrary",), vmem_limit_bytes=VMEM_LIMIT),
        name="adaln",
    )(c, w_ada, b_ada.reshape(1, n))


def _inproj_kernel(x_ref, shift_ref, scale_ref, gn_ref, w_ref, convw_ref, tri_ref,
                   gq_ref, gk_ref, bf_ref,
                   qkva_ref, za_ref, qf_ref, kf_ref, vft_ref, zf_ref, ga_ref, gf_ref, gates_ref, stats_ref,
                   carry_ref, xbuf_q, xbuf_k, xbuf_v, *, tm, d_model):
    i = pl.program_id(1)
    xbufs = (xbuf_q, xbuf_k, xbuf_v)

    @pl.when(i == 0)
    def _():
        carry_ref[...] = jnp.zeros_like(carry_ref)
        for xb in xbufs:
            xb[0:SUBLANES, :] = jnp.zeros((SUBLANES, W_MIX), F32)

    x = x_ref[...]
    ms = jnp.mean(x * x, axis=-1, keepdims=True)
    h = x * lax.rsqrt(ms + EPS) * gn_ref[...]
    h = h * (1.0 + scale_ref[...]) + shift_ref[...]
    hb = h.astype(BF16)

    def proj(lo, width):
        return jnp.dot(hb, w_ref[:, lo:lo + width], preferred_element_type=F32)

    NT = MXU_TILE
    RC = tm // 4

    def qkva_tile(t):
        j, half = divmod(t, W_MIX // NT)
        xbufs[j][SUBLANES:SUBLANES + tm, half * NT:(half + 1) * NT] = proj(_C_QKVA + t * NT, NT)

    def conv_silu(j, rc):
        xbuf = xbufs[j]
        cols = slice(j * W_MIX, (j + 1) * W_MIX)
        r0 = SUBLANES - 3 + rc * RC
        y = xbuf[r0:r0 + RC, :] * convw_ref[0:1, cols]
        for tap in range(1, CONV_WIDTH):
            y = y + xbuf[r0 + tap:r0 + tap + RC, :] * convw_ref[tap:tap + 1, cols]
        qkva_ref[rc * RC:(rc + 1) * RC, cols] = _silu(y).astype(BF16)
        if rc == tm // RC - 1:
            xbuf[0:SUBLANES, :] = xbuf[tm:tm + SUBLANES, :]

    def plain(out_ref, lo, t):
        out_ref[:, t * NT:(t + 1) * NT] = proj(lo + t * NT, NT).astype(BF16)

    def vft_tile(t):
        vft_ref[t * NT:(t + 1) * NT, :] = proj(_C_VF + t * NT, NT).astype(BF16).T

    qkva_tile(0)
    qkva_tile(1)
    for t in (2, 3, 4, 5):
        qkva_tile(t)
        conv_silu(0, t - 2)
    g = proj(_C_GA + 2 * d_model, LANES)
    conv_silu(1, 0)
    qk_tiles = {}
    for n, (name, lo) in enumerate((("q", _C_QF), ("q", _C_QF + NT), ("k", _C_KF), ("k", _C_KF + NT))):
        qk_tiles[(name, n % 2)] = proj(lo, NT)
        if n < 3:
            conv_silu(1, n + 1)

    gates_ref[...] = g
    logf = _log_sigmoid(g + bf_ref[...])
    hi, mid, lo = _split3_bf16(logf)
    cs = jnp.dot(tri_ref[...], jnp.concatenate([hi, mid, lo], axis=1).astype(BF16),
                 preferred_element_type=F32)
    csum = (cs[:, :LANES] + cs[:, LANES:2 * LANES]) + cs[:, 2 * LANES:] + carry_ref[...]
    carry_ref[...] = csum[tm - 1:tm, :]

    for n, (ref, lo, t) in enumerate(((za_ref, _C_ZA, 0), (za_ref, _C_ZA, 1),
                                      (zf_ref, _C_ZF, 0), (zf_ref, _C_ZF, 1))):
        plain(ref, lo, t)
        conv_silu(2, n)

    HR = tm // 2
    lane = lax.broadcasted_iota(jnp.int32, (HR, LANES), 1)
    one = jnp.where(lane < 3, 0.0, jnp.where(lane < 6, 1.0, 0.0))
    one_k = jnp.where(lane < 3, 1.0, 0.0)

    def qk_epilogue(hd, half):
        rows = slice(half * HR, (half + 1) * HR)
        sl = slice((hd % 2) * HEAD_DIM, (hd % 2 + 1) * HEAD_DIM)
        qh = qk_tiles[("q", hd // 2)][rows, sl]
        kh = qk_tiles[("k", hd // 2)][rows, sl]
        qn = qh * lax.rsqrt(jnp.mean(qh * qh, axis=-1, keepdims=True) + EPS) * gq_ref[...]
        kn = kh * lax.rsqrt(jnp.mean(kh * kh, axis=-1, keepdims=True) + EPS) * gk_ref[...]
        ch = jnp.broadcast_to(csum[rows, _L_F + hd:_L_F + hd + 1], (HR, LANES)) * LOG2E
        hi, mid, lo = _split3_bf16(ch)
        aug_q = jnp.where(lane == 0, hi, jnp.where(lane == 1, mid, jnp.where(lane == 2, lo, one)))
        aug_k = jnp.where(lane == 3, -hi, jnp.where(lane == 4, -mid, jnp.where(lane == 5, -lo, one_k)))
        base = hd * 2 * HEAD_DIM
        qb = (qn * (HEAD_DIM ** -0.5 * LOG2E)).astype(BF16)
        kb = kn.astype(BF16)
        qf_ref[rows, base:base + HEAD_DIM] = qb
        qf_ref[rows, base + HEAD_DIM:base + 2 * HEAD_DIM] = aug_q.astype(BF16)
        kf_ref[rows, base:base + HEAD_DIM] = kb
        kf_ref[rows, base + HEAD_DIM:base + 2 * HEAD_DIM] = aug_k.astype(BF16)
        for name, val in (("q", qb), ("k", kb)):
            v32 = val.astype(F32)
            sq_max[(name, hd, half)] = jnp.max(jnp.sum(v32 * v32, axis=-1, keepdims=True),
                                               axis=0, keepdims=True)

    sq_max = {}
    m_steps = ([functools.partial(vft_tile, t) for t in range(W_MIX // NT)]
               + [functools.partial(plain, ga_ref, _C_GA, t) for t in range(d_model // NT)]
               + [functools.partial(plain, gf_ref, _C_GA + d_model, t) for t in range(d_model // NT)])
    v_steps = [functools.partial(qk_epilogue, hd, half) for hd in range(N_HEADS) for half in range(2)]
    for n, m_step in enumerate(m_steps):
        m_step()
        if n < len(v_steps):
            v_steps[n]()

    lane1 = lax.broadcasted_iota(jnp.int32, (1, LANES), 1)
    norm_rows = []
    for name in ("q", "k"):
        row = jnp.zeros((1, LANES), F32)
        for hd in range(N_HEADS):
            row = jnp.where(lane1 == hd, jnp.maximum(sq_max[(name, hd, 0)], sq_max[(name, hd, 1)]), row)
        norm_rows.append(row)
    stats_ref[...] = jnp.concatenate(
        [csum[0:1, :] * LOG2E, csum[tm - 1:tm, :] * LOG2E] + norm_rows
        + [jnp.zeros((SUBLANES - 4, LANES), F32)], axis=0)


def _inproj(x, shift, scale, g_norm, w_perm, conv_w, g_q, g_k, bf_lane, tm):
    B, T, D = x.shape
    n_w = w_perm.shape[1]
    tri = jnp.tril(jnp.ones((tm, tm), BF16))
    tok = lambda width: pl.BlockSpec((None, tm, width), lambda b, i: (b, i, 0))
    per_b = pl.BlockSpec((None, 1, D), lambda b, i: (b, 0, 0))
    const = lambda shape: pl.BlockSpec(shape, lambda b, i: (0,) * len(shape))
    bf16_out = lambda w: (tok(w), jax.ShapeDtypeStruct((B, T, w), BF16))
    outs = [bf16_out(3 * W_MIX), bf16_out(W_MIX), bf16_out(2 * W_MIX), bf16_out(2 * W_MIX),
            (pl.BlockSpec((None, None, W_MIX, tm), lambda b, i: (b, i, 0, 0)),
             jax.ShapeDtypeStruct((B, T // tm, W_MIX, tm), BF16)),
            bf16_out(W_MIX), bf16_out(D), bf16_out(D),
            (tok(LANES), jax.ShapeDtypeStruct((B, T, LANES), F32)),
            (pl.BlockSpec((None, None, SUBLANES, LANES), lambda b, i: (b, i, 0, 0)),
             jax.ShapeDtypeStruct((B, T // tm, SUBLANES, LANES), F32))]
    return pl.pallas_call(
        functools.partial(_inproj_kernel, tm=tm, d_model=D),
        grid=(B, T // tm),
        in_specs=[tok(D), per_b, per_b, const((1, D)), const((D, n_w)),
                  const((CONV_WIDTH, 3 * W_MIX)), const((tm, tm)),
                  const((1, HEAD_DIM)), const((1, HEAD_DIM)), const((1, LANES))],
        out_specs=[o[0] for o in outs],
        out_shape=[o[1] for o in outs],
        scratch_shapes=[pltpu.VMEM((1, LANES), F32)] + [pltpu.VMEM((tm + SUBLANES, W_MIX), F32)] * 3,
        compiler_params=pltpu.CompilerParams(dimension_semantics=("parallel", "arbitrary"),
                                             vmem_limit_bytes=VMEM_LIMIT),
        name="inproj",
    )(x, shift, scale, g_norm, w_perm, conv_w, tri, g_q, g_k, bf_lane)


def _gdn_masks():
    r = lax.broadcasted_iota(jnp.int32, (GDN_BLOCK, GDN_BLOCK), 0)
    c = lax.broadcasted_iota(jnp.int32, (GDN_BLOCK, GDN_BLOCK), 1)
    same = (r // GDN_CHUNK) == (c // GDN_CHUNK)
    lower = jnp.logical_and(same, r >= c)
    f_lower = jnp.where(lower, 1.0, 0.0)
    f_same = jnp.where(same, 1.0, 0.0)
    f_cum = jnp.concatenate([f_lower, f_same], axis=0).astype(BF16)
    f_upper = jnp.where(jnp.logical_and(same, r <= c), 1.0, 0.0).astype(BF16)
    eye = jnp.where(r == c, 1.0, 0.0).astype(F32)
    nm_lower = jnp.where(lower, 0.0, -MASK_BIG).astype(F32)
    nm_strict = jnp.where(jnp.logical_and(same, r > c), 0.0, -MASK_BIG).astype(F32)
    return f_cum, f_upper, eye, nm_lower, nm_strict


def _gdn_kernel(qkv_ref, slab_ref, gt_ref, acol_ref, dcol_ref, arow_ref, drow_ref,
                fcum_ref, fupper_ref, eye_ref, nml_ref, nms_ref,
                za_ref, gout_ref, o_ref, s_ref, *, tt):
    i = pl.program_id(1)
    C = GDN_CHUNK
    BLK = GDN_BLOCK
    n_chunks = tt // C
    shift = C.bit_length() - 1

    @pl.when(i == 0)
    def _():
        s_ref[...] = jnp.zeros_like(s_ref)

    f_cum = fcum_ref[...]
    f_upper = fupper_ref[...]
    eye = eye_ref[...]
    nm_lower = nml_ref[...]
    nm_strict = nms_ref[...]
    n_blk = tt // BLK
    per_blk = BLK // C
    n_fac = shift - 1
    assert per_blk == N_HEADS

    def split2(v):
        hi = v.astype(BF16)
        return hi, (v - hi.astype(F32)).astype(BF16)

    def col(slab_, lane):
        return jnp.broadcast_to(slab_[:, lane:lane + 1], (slab_.shape[0], HEAD_DIM))

    gates = {}

    def prep_gates(sb):
        rows = slice(sb * BLK, (sb + 1) * BLK)
        slab = slab_ref[rows, :]
        g_col = -jnp.exp(acol_ref[...]) * _softplus(slab + dcol_ref[...])
        g_row = -jnp.exp(arow_ref[...]) * _softplus(gt_ref[:, rows] + drow_ref[...])
        g_hi, g_lo = split2(g_col)
        cum = jnp.dot(f_cum, jnp.concatenate([g_hi, g_lo], axis=1), preferred_element_type=F32)
        cum = cum[:, :LANES] + cum[:, LANES:]
        gc_col, gl_col = cum[:BLK], cum[BLK:]
        r_hi, r_lo = split2(g_row)
        cr = jnp.dot(jnp.concatenate([r_hi, r_lo], axis=0), f_upper, preferred_element_type=F32)
        gates[sb] = dict(beta=_sigmoid(slab), gc_col=gc_col, gc_row=cr[:SUBLANES] + cr[SUBLANES:],
                         e_gc=jnp.exp(gc_col), e_rem=jnp.exp(gl_col - gc_col), e_tot=jnp.exp(gl_col))

    qk_m, rhs_uw, qg_m, kd_m, etot_m, p_m, x_m, u_m, w_m = {}, {}, {}, {}, {}, {}, {}, {}, {}

    def prep_head(sb, hd):
        pr = (sb, hd)
        gt = gates[sb]
        rows = slice(sb * BLK, (sb + 1) * BLK)
        qh = qkv_ref[rows, hd * HEAD_DIM:(hd + 1) * HEAD_DIM].astype(F32)
        kh = qkv_ref[rows, W_MIX + hd * HEAD_DIM:W_MIX + (hd + 1) * HEAD_DIM].astype(F32)
        vh = qkv_ref[rows, 2 * W_MIX + hd * HEAD_DIM:2 * W_MIX + (hd + 1) * HEAD_DIM].astype(F32)
        qn = qh * lax.rsqrt(jnp.sum(qh * qh, axis=-1, keepdims=True) + EPS) * (HEAD_DIM ** -0.5)
        kn = kh * lax.rsqrt(jnp.sum(kh * kh, axis=-1, keepdims=True) + EPS)
        bh = col(gt["beta"], _L_B + hd)
        e_gc = col(gt["e_gc"], _L_A + hd)
        kb = kn * bh
        rhs_uw[pr] = jnp.concatenate([vh * bh, kb * e_gc], axis=1).astype(BF16)
        qg_m[pr] = (qn * e_gc).astype(BF16)
        kd_m[pr] = (kn * col(gt["e_rem"], _L_A + hd)).astype(BF16)
        etot_m[pr] = col(gt["e_tot"], _L_A + hd)
        kq = jnp.dot(jnp.concatenate([kb, qn], axis=0).astype(BF16), kn.T.astype(BF16),
                     preferred_element_type=F32)
        diff = gt["gc_col"][:, _L_A + hd:_L_A + hd + 1] - gt["gc_row"][hd:hd + 1, :]
        a = kq[:BLK] * jnp.exp(diff + nm_strict)
        qk_m[pr] = (kq[BLK:] * jnp.exp(diff + nm_lower)).astype(BF16)
        ab = a.astype(BF16)
        p_m[pr] = eye - a
        x_m[pr] = jnp.dot(ab, ab, preferred_element_type=F32)

    def inv_iter(sb, j):
        for hd in range(N_HEADS):
            pr = (sb, hd)
            xb = x_m[pr].astype(BF16)
            if j + 1 < n_fac:
                px = jnp.dot(jnp.concatenate([p_m[pr].astype(BF16), xb], axis=0), xb,
                             preferred_element_type=F32)
                p_m[pr] = p_m[pr] + px[:BLK]
                x_m[pr] = px[BLK:]
            else:
                p_m[pr] = p_m[pr] + jnp.dot(p_m[pr].astype(BF16), xb, preferred_element_type=F32)

    def apply_inverse(sb):
        for hd in range(N_HEADS):
            pr = (sb, hd)
            uw = jnp.dot(p_m[pr].astype(BF16), rhs_uw[pr], preferred_element_type=F32)
            u_m[pr] = uw[:, :HEAD_DIM]
            w_m[pr] = uw[:, HEAD_DIM:].astype(BF16)

    state = [s_ref[hd] for hd in range(N_HEADS)]
    vnew_parts, qs_parts = {}, {}

    def state_chunk(sb, cb):
        rows = slice(cb * C, (cb + 1) * C)
        for hd in range(N_HEADS):
            pr = (sb, hd)
            rr = jnp.dot(jnp.concatenate([w_m[pr][rows], qg_m[pr][rows]], axis=0),
                         state[hd].astype(BF16), preferred_element_type=F32)
            vnew = u_m[pr][rows] - rr[:C]
            qs_parts.setdefault(pr, []).append(rr[C:])
            vnew_parts.setdefault(pr, []).append(vnew)
            upd = lax.dot_general(kd_m[pr][rows], vnew.astype(BF16), (((0,), (0,)), ((), ())),
                                  preferred_element_type=F32)
            state[hd] = state[hd] * etot_m[pr][cb * C:cb * C + 1, :] + upd

    def out_head(sb, hd):
        pr = (sb, hd)
        rows = slice(sb * BLK, (sb + 1) * BLK)
        sl = slice(hd * HEAD_DIM, (hd + 1) * HEAD_DIM)
        vn = jnp.concatenate(vnew_parts[pr], axis=0).astype(BF16)
        o = jnp.concatenate(qs_parts[pr], axis=0) + jnp.dot(qk_m[pr], vn, preferred_element_type=F32)
        on = o * lax.rsqrt(jnp.mean(o * o, axis=-1, keepdims=True) + EPS) * gout_ref[...]
        o_ref[rows, sl] = (on * _silu(za_ref[rows, sl].astype(F32))).astype(BF16)

    inv_share = [[] for _ in range(per_blk)]
    for j in range(n_fac):
        inv_share[max(0, j - (n_fac - per_blk))].append(j)
    for t in range(-2, n_blk + 1):
        if 0 <= t + 2 < n_blk:
            prep_gates(t + 2)
        for slot in range(per_blk):
            if 0 <= t < n_blk:
                state_chunk(t, slot)
            if 0 <= t + 1 < n_blk:
                for j in inv_share[slot]:
                    inv_iter(t + 1, j)
                if slot == per_blk - 1:
                    apply_inverse(t + 1)
            if 0 <= t + 2 < n_blk:
                prep_head(t + 2, slot)
            if 0 <= t - 1 < n_blk:
                out_head(t - 1, slot)
    for hd in range(N_HEADS):
        s_ref[hd] = state[hd]


def _gdn(qkva, slab, gates_t, a_log, dt_bias, za, g_out, tt):
    B, T, _ = qkva.shape
    tok = lambda width: pl.BlockSpec((None, tt, width), lambda b, i: (b, i, 0))
    const = lambda shape: pl.BlockSpec(shape, lambda b, i: (0,) * len(shape))
    lane_vec = lambda v: jnp.zeros((1, LANES), F32).at[0, _L_A:_L_A + N_HEADS].set(v)
    sub_vec = lambda v: jnp.zeros((SUBLANES, 1), F32).at[0:N_HEADS, 0].set(v)
    masks = _gdn_masks()
    return pl.pallas_call(
        functools.partial(_gdn_kernel, tt=tt),
        grid=(B, T // tt),
        in_specs=[tok(3 * W_MIX), tok(LANES),
                  pl.BlockSpec((None, SUBLANES, tt), lambda b, i: (b, 0, i)),
                  const((1, LANES)), const((1, LANES)), const((SUBLANES, 1)), const((SUBLANES, 1))]
                 + [const(m.shape) for m in masks]
                 + [tok(W_MIX), const((1, HEAD_DIM))],
        out_specs=tok(W_MIX),
        out_shape=jax.ShapeDtypeStruct((B, T, W_MIX), BF16),
        scratch_shapes=[pltpu.VMEM((N_HEADS, HEAD_DIM, HEAD_DIM), F32)],
        compiler_params=pltpu.CompilerParams(dimension_semantics=("parallel", "arbitrary"),
                                             vmem_limit_bytes=VMEM_LIMIT),
        name="gdn",
    )(qkva, slab, gates_t, lane_vec(a_log), lane_vec(dt_bias), sub_vec(a_log), sub_vec(dt_bias),
      *masks, za, g_out.reshape(1, HEAD_DIM))


def _fox_kernel(fp_ref, q_ref, k_ref, vt_ref, zf_ref, o_ref, m_ref, acc_ref, sa_ref, sb_ref, qt_ref,
                *, tq, tk, n_q):
    qi = pl.program_id(2)
    p0 = fp_ref[(pl.program_id(0) * N_HEADS + pl.program_id(1)) * n_q + qi]
    m_ref[...] = jnp.full_like(m_ref, NEG_INF)
    acc_ref[...] = jnp.zeros_like(acc_ref)
    qt_ref[...] = q_ref[...].T
    ones_rows = jnp.ones((FOX_SUM_ROWS, tk), BF16)
    halves = (slice(0, tk), slice(tk, 2 * tk))

    def scores_into(buf, kb, which=(0, 1)):
        k = k_ref[pl.ds(pl.multiple_of(kb * tk, tk), tk), :]
        for h in which:
            buf[h] = jnp.dot(k, qt_ref[:, halves[h]], preferred_element_type=F32)

    def consume(buf, kb, modes):
        vt = jnp.concatenate([vt_ref[kb], ones_rows], axis=0)
        live = [h for h in (0, 1) if modes[h] != "skip"]
        pts, alphas = {}, {}
        for h in live:
            st = buf[h]
            if modes[h] == "diag":
                krow = lax.broadcasted_iota(jnp.int32, (tk, tk), 0)
                qcol = lax.broadcasted_iota(jnp.int32, (tk, tk), 1)
                st = jnp.where(qcol >= krow, st, NEG_INF)
            m_prev = m_ref[:, halves[h]]
            m_new = jnp.maximum(m_prev, jnp.max(st, axis=0, keepdims=True))
            alphas[h] = jnp.exp2(m_prev - m_new)
            pts[h] = jnp.exp2(st - m_new).astype(BF16)
            m_ref[:, halves[h]] = m_new
        for h in live:
            acc_ref[:, halves[h]] = (alphas[h] * acc_ref[:, halves[h]]
                                     + jnp.dot(vt, pts[h], preferred_element_type=F32))

    full = ("full", "full")
    scores_into(sa_ref, 2 * p0)

    def pair(p, carry):
        kb = 2 * p
        scores_into(sb_ref, kb + 1)
        consume(sa_ref, kb, full)
        scores_into(sa_ref, kb + 2)
        consume(sb_ref, kb + 1, full)
        return carry

    lax.fori_loop(p0, qi, pair, 0)
    scores_into(sb_ref, 2 * qi + 1, which=(1,))
    consume(sa_ref, 2 * qi, ("diag", "full"))
    consume(sb_ref, 2 * qi + 1, ("skip", "diag"))

    ot = acc_ref[0:HEAD_DIM, :] * (1.0 / acc_ref[HEAD_DIM:HEAD_DIM + 1, :])
    o_ref[...] = (ot.T * _silu(zf_ref[...].astype(F32))).astype(BF16)


def _fox_first_pair(stats, tk, n_q):
    c_first = stats[:, :, 0, _L_F:_L_F + N_HEADS]
    c_last = stats[:, :, 1, _L_F:_L_F + N_HEADS]
    qk_bound = jnp.sqrt(jnp.max(stats[:, :, 2, :N_HEADS], axis=1) * jnp.max(stats[:, :, 3, :N_HEADS], axis=1))
    n_kb = stats.shape[1]
    upper = (2.0 * qk_bound[:, None, None, :] + c_first[:, 0::2, None, :][:, :n_q]
             - c_last[:, None, :, :])
    kb = jnp.arange(n_kb, dtype=jnp.int32)[None, None, :, None]
    diag = 2 * jnp.arange(n_q, dtype=jnp.int32)[None, :, None, None]
    needed = jnp.logical_or(upper >= -FOX_SKIP_LOG2, kb >= diag)
    first_block = jnp.min(jnp.where(needed, kb, n_kb), axis=2)
    return jnp.right_shift(first_block, 1).transpose(0, 2, 1).reshape(-1)


def _fox(qf_aug, kf_aug, vft, zf, stats, tk):
    B, T, _ = zf.shape
    tq = 2 * tk
    assert T % tq == 0
    n_q = T // tq
    first_pair = _fox_first_pair(stats, tk, n_q)
    grid_spec = pltpu.PrefetchScalarGridSpec(
        num_scalar_prefetch=1,
        grid=(B, N_HEADS, n_q),
        in_specs=[pl.BlockSpec((None, tq, 2 * HEAD_DIM), lambda b, h, i, fp: (b, i, h)),
                  pl.BlockSpec((None, T, 2 * HEAD_DIM), lambda b, h, i, fp: (b, 0, h)),
                  pl.BlockSpec((None, T // tk, HEAD_DIM, tk), lambda b, h, i, fp: (b, 0, h, 0)),
                  pl.BlockSpec((None, tq, HEAD_DIM), lambda b, h, i, fp: (b, i, h))],
        out_specs=pl.BlockSpec((None, tq, HEAD_DIM), lambda b, h, i, fp: (b, i, h)),
        scratch_shapes=[pltpu.VMEM((1, tq), F32), pltpu.VMEM((HEAD_DIM + FOX_SUM_ROWS, tq), F32),
                        pltpu.VMEM((2, tk, tk), F32), pltpu.VMEM((2, tk, tk), F32),
                        pltpu.VMEM((2 * HEAD_DIM, tq), BF16)])
    return pl.pallas_call(
        functools.partial(_fox_kernel, tq=tq, tk=tk, n_q=n_q),
        grid_spec=grid_spec,
        out_shape=jax.ShapeDtypeStruct((B, T, W_MIX), BF16),
        compiler_params=pltpu.CompilerParams(dimension_semantics=("parallel", "parallel", "arbitrary"),
                                             vmem_limit_bytes=VMEM_LIMIT),
        name="fox",
    )(first_pair, qf_aug, kf_aug, vft, zf)


def _outproj_kernel(oa_ref, of_ref, ga_ref, gf_ref, x_ref, gate_ref, wa_ref, wf_ref, wo_ref, o_ref):
    ya = jnp.dot(oa_ref[...], wa_ref[...], preferred_element_type=F32)
    yf = jnp.dot(of_ref[...], wf_ref[...], preferred_element_type=F32)
    merged = _sigmoid(ga_ref[...].astype(F32)) * ya + _sigmoid(gf_ref[...].astype(F32)) * yf
    res = jnp.dot(merged.astype(BF16), wo_ref[...], preferred_element_type=F32)
    o_ref[...] = x_ref[...] + gate_ref[...] * res


def _outproj(oa, of, ga, gf, x, gate, wa, wf, wo, tm):
    B, T, D = x.shape
    tok = lambda width: pl.BlockSpec((None, tm, width), lambda b, i: (b, i, 0))
    const = lambda shape: pl.BlockSpec(shape, lambda b, i: (0,) * len(shape))
    return pl.pallas_call(
        _outproj_kernel,
        grid=(B, T // tm),
        in_specs=[tok(W_MIX), tok(W_MIX), tok(D), tok(D), tok(D),
                  pl.BlockSpec((None, 1, D), lambda b, i: (b, 0, 0)),
                  const((W_MIX, D)), const((W_MIX, D)), const((D, D))],
        out_specs=tok(D),
        out_shape=jax.ShapeDtypeStruct((B, T, D), F32),
        compiler_params=pltpu.CompilerParams(dimension_semantics=("parallel", "parallel"),
                                             vmem_limit_bytes=VMEM_LIMIT),
        name="outproj",
    )(oa, of, ga, gf, x, gate, wa, wf, wo)


def _permute_w_in(w_in, d_model):
    o = 0
    cols = {}
    for name, width in (("qkva", 3 * W_MIX), ("a", N_HEADS), ("b", N_HEADS), ("za", W_MIX),
                        ("qf", W_MIX), ("kf", W_MIX), ("vf", W_MIX), ("f", N_HEADS), ("zf", W_MIX),
                        ("ga", d_model), ("gf", d_model)):
        cols[name] = w_in[:, o:o + width]
        o += width
    gates = jnp.concatenate([cols["a"], cols["b"], cols["f"],
                             jnp.zeros((w_in.shape[0], LANES - 3 * N_HEADS), w_in.dtype)], axis=1)
    return jnp.concatenate([cols["qkva"], cols["za"], cols["qf"], cols["kf"], cols["vf"], cols["zf"],
                            cols["ga"], cols["gf"], gates], axis=1).astype(BF16)


def kernel(x, c, w_ada, b_ada, g_norm, w_in, conv_w, A_log, dt_bias, g_gdn_out,
           g_q_fox, g_k_fox, b_f, w_o_gdn, w_o_fox, w_out):
    B, T, D = x.shape
    tm = tk = min(512, T)
    tt = min(4 * GDN_BLOCK, T)

    mod = _adaln(c, w_ada, b_ada)
    shift, scale, gate = (mod[:, j * D:(j + 1) * D].reshape(B, 1, D) for j in range(3))

    w_perm = _permute_w_in(w_in, D)
    bf_lane = jnp.zeros((1, LANES), F32).at[0, _L_F:_L_F + N_HEADS].set(b_f)
    (qkva, za, qf_aug, kf_aug, vft, zf, ga, gf, slab, stats) = _inproj(
        x, shift, scale, g_norm.reshape(1, D), w_perm, conv_w,
        g_q_fox.reshape(1, HEAD_DIM), g_k_fox.reshape(1, HEAD_DIM), bf_lane, tm)

    gates_t = jnp.transpose(slab[:, :, :SUBLANES], (0, 2, 1))
    oa = _gdn(qkva, slab, gates_t, A_log, dt_bias, za, g_gdn_out, tt)
    of = _fox(qf_aug, kf_aug, vft, zf, stats, tk)
    return _outproj(oa, of, ga, gf, x, gate, w_o_gdn.astype(BF16), w_o_fox.astype(BF16),
                    w_out.astype(BF16), tm)
```

```python
import functools

import jax
import jax.numpy as jnp
from jax import lax
from jax.experimental import pallas as pl
from jax.experimental.pallas import tpu as pltpu

F32 = jnp.float32
BF16 = jnp.bfloat16
EPS = 1e-6
HEAD_DIM = 128
N_HEADS = 4
W_MIX = N_HEADS * HEAD_DIM
CONV_WIDTH = 4
GDN_CHUNK = 64
GDN_BLOCK = 256
MASK_BIG = 1e30
LANES = 128
MXU_TILE = 256
SUBLANES = 8
VMEM_LIMIT = 56 * 1024 * 1024
HI = lax.Precision.HIGHEST
NEG_INF = float("-inf")
LOG2E = 1.4426950408889634
FOX_SUM_ROWS = 16
FOX_SKIP_LOG2 = 152.0

_C_QKVA = 0
_C_ZA = 3 * W_MIX
_C_QF = _C_ZA + W_MIX
_C_KF = _C_QF + W_MIX
_C_VF = _C_KF + W_MIX
_C_ZF = _C_VF + W_MIX
_C_GA = _C_ZF + W_MIX
_L_A, _L_B, _L_F = 0, N_HEADS, 2 * N_HEADS


def _softplus(z):
    return jnp.maximum(z, 0.0) + jnp.log(1.0 + jnp.exp(-jnp.abs(z)))


def _log_sigmoid(z):
    return jnp.minimum(z, 0.0) - jnp.log(1.0 + jnp.exp(-jnp.abs(z)))


def _sigmoid(z):
    return 1.0 / (1.0 + jnp.exp(-z))


def _silu(z):
    return z * _sigmoid(z)


def _split3_bf16(c):
    hi = c.astype(BF16).astype(F32)
    r = c - hi
    mid = r.astype(BF16).astype(F32)
    lo = (r - mid).astype(BF16).astype(F32)
    return hi, mid, lo


def _adaln_kernel(c_ref, w_ref, b_ref, o_ref):
    o_ref[...] = jnp.dot(c_ref[...], w_ref[...], precision=HI, preferred_element_type=F32) + b_ref[...]


def _adaln(c, w_ada, b_ada):
    B, D = c.shape
    n = w_ada.shape[1]
    bn = D
    return pl.pallas_call(
        _adaln_kernel,
        grid=(n // bn,),
        in_specs=[pl.BlockSpec((B, D), lambda j: (0, 0)),
                  pl.BlockSpec((D, bn), lambda j: (0, j)),
                  pl.BlockSpec((1, bn), lambda j: (0, j))],
        out_specs=pl.BlockSpec((B, bn), lambda j: (0, j)),
        out_shape=jax.ShapeDtypeStruct((B, n), F32),
        compiler_params=pltpu.CompilerParams(dimension_semantics=("arbitrary",), vmem_limit_bytes=VMEM_LIMIT),
        name="adaln",
    )(c, w_ada, b_ada.reshape(1, n))


def _inproj_kernel(x_ref, shift_ref, scale_ref, gn_ref, w_ref, convw_ref, tri_ref,
                   gq_ref, gk_ref, bf_ref,
                   qkva_ref, za_ref, qf_ref, kf_ref, vft_ref, zf_ref, ga_ref, gf_ref, gates_ref, stats_ref,
                   carry_ref, xbuf_q, xbuf_k, xbuf_v, hbuf, gbuf, *, tm, d_model):
    i = pl.program_id(1)
    xbufs = (xbuf_q, xbuf_k, xbuf_v)

    @pl.when(i == 0)
    def _():
        carry_ref[...] = jnp.zeros_like(carry_ref)
        for xb in xbufs:
            xb[0:SUBLANES, :] = jnp.zeros((SUBLANES, W_MIX), F32)

    x = x_ref[...]
    ms = jnp.mean(x * x, axis=-1, keepdims=True)
    h = x * lax.rsqrt(ms + EPS) * gn_ref[...]
    h = h * (1.0 + scale_ref[...]) + shift_ref[...]
    hb = h.astype(BF16)

    def chunk_permuted(ref):
        return jnp.concatenate([ref[pl.ds(c * GDN_CHUNK + j, SUBLANES, stride=SUBLANES), :]
                                for c in range(tm // GDN_CHUNK) for j in range(SUBLANES)], axis=0)

    for kg in range(d_model // LANES):
        hbuf[kg] = h[:, kg * LANES:(kg + 1) * LANES]
    hbp = jnp.concatenate([chunk_permuted(hbuf.at[kg]) for kg in range(d_model // LANES)],
                          axis=1).astype(BF16)

    def proj(lo, width):
        return jnp.dot(hb, w_ref[:, lo:lo + width], preferred_element_type=F32)

    NT = MXU_TILE
    RC = tm // 4

    def qkva_tile(t):
        j, half = divmod(t, W_MIX // NT)
        xbufs[j][SUBLANES:SUBLANES + tm, half * NT:(half + 1) * NT] = jnp.dot(
            hbp, w_ref[:, _C_QKVA + t * NT:_C_QKVA + (t + 1) * NT], preferred_element_type=F32)

    def conv_silu(j, rc):
        xbuf = xbufs[j]
        cols = slice(j * W_MIX, (j + 1) * W_MIX)
        n_tail = CONV_WIDTH - 1
        for c in range(rc * RC // GDN_CHUNK, (rc + 1) * RC // GDN_CHUNK):
            b0 = SUBLANES + c * GDN_CHUNK
            wrapped = []
            for g in range(SUBLANES - n_tail, SUBLANES):
                prev = g if c == 0 else b0 - GDN_CHUNK + SUBLANES * g + SUBLANES - 1
                wrapped += [xbuf[prev:prev + 1, :], xbuf[b0 + SUBLANES * g:b0 + SUBLANES * (g + 1) - 1, :]]
            ext = jnp.concatenate(wrapped + [xbuf[b0:b0 + GDN_CHUNK, :]], axis=0)
            y = ext[0:GDN_CHUNK] * convw_ref[0:1, cols]
            for tap in range(1, CONV_WIDTH):
                y = y + ext[SUBLANES * tap:SUBLANES * tap + GDN_CHUNK] * convw_ref[tap:tap + 1, cols]
            qkva_ref[c * GDN_CHUNK:(c + 1) * GDN_CHUNK, cols] = _silu(y).astype(BF16)
        if rc == tm // RC - 1:
            last = SUBLANES + tm - GDN_CHUNK
            for g in range(SUBLANES - n_tail, SUBLANES):
                r = last + SUBLANES * g + SUBLANES - 1
                xbuf[g:g + 1, :] = xbuf[r:r + 1, :]

    def plain(out_ref, lo, t):
        out_ref[:, t * NT:(t + 1) * NT] = proj(lo + t * NT, NT).astype(BF16)

    def vft_tile(t):
        vft_ref[t * NT:(t + 1) * NT, :] = proj(_C_VF + t * NT, NT).astype(BF16).T

    qkva_tile(0)
    qkva_tile(1)
    for t in (2, 3, 4, 5):
        qkva_tile(t)
        conv_silu(0, t - 2)
    g = proj(_C_GA + 2 * d_model, LANES)
    conv_silu(1, 0)
    qk_tiles = {}
    for n, (name, lo) in enumerate((("q", _C_QF), ("q", _C_QF + NT), ("k", _C_KF), ("k", _C_KF + NT))):
        qk_tiles[(name, n % 2)] = proj(lo, NT)
        if n < 3:
            conv_silu(1, n + 1)

    gbuf[...] = g
    gates_ref[...] = chunk_permuted(gbuf)
    logf = _log_sigmoid(g + bf_ref[...])
    hi, mid, lo = _split3_bf16(logf)
    cs = jnp.dot(tri_ref[...], jnp.concatenate([hi, mid, lo], axis=1).astype(BF16),
                 preferred_element_type=F32)
    csum = (cs[:, :LANES] + cs[:, LANES:2 * LANES]) + cs[:, 2 * LANES:] + carry_ref[...]
    carry_ref[...] = csum[tm - 1:tm, :]

    for n, (ref, lo, t) in enumerate(((za_ref, _C_ZA, 0), (za_ref, _C_ZA, 1),
                                      (zf_ref, _C_ZF, 0), (zf_ref, _C_ZF, 1))):
        plain(ref, lo, t)
        conv_silu(2, n)

    HR = tm // 2
    lane = lax.broadcasted_iota(jnp.int32, (HR, LANES), 1)
    one = jnp.where(lane < 3, 0.0, jnp.where(lane < 6, 1.0, 0.0))
    one_k = jnp.where(lane < 3, 1.0, 0.0)

    def qk_epilogue(hd, half):
        rows = slice(half * HR, (half + 1) * HR)
        sl = slice((hd % 2) * HEAD_DIM, (hd % 2 + 1) * HEAD_DIM)
        qh = qk_tiles[("q", hd // 2)][rows, sl]
        kh = qk_tiles[("k", hd // 2)][rows, sl]
        qn = qh * lax.rsqrt(jnp.mean(qh * qh, axis=-1, keepdims=True) + EPS) * gq_ref[...]
        kn = kh * lax.rsqrt(jnp.mean(kh * kh, axis=-1, keepdims=True) + EPS) * gk_ref[...]
        ch = jnp.broadcast_to(csum[rows, _L_F + hd:_L_F + hd + 1], (HR, LANES)) * LOG2E
        hi, mid, lo = _split3_bf16(ch)
        aug_q = jnp.where(lane == 0, hi, jnp.where(lane == 1, mid, jnp.where(lane == 2, lo, one)))
        aug_k = jnp.where(lane == 3, -hi, jnp.where(lane == 4, -mid, jnp.where(lane == 5, -lo, one_k)))
        base = hd * 2 * HEAD_DIM
        qb = (qn * (HEAD_DIM ** -0.5 * LOG2E)).astype(BF16)
        kb = kn.astype(BF16)
        qf_ref[rows, base:base + HEAD_DIM] = qb
        qf_ref[rows, base + HEAD_DIM:base + 2 * HEAD_DIM] = aug_q.astype(BF16)
        kf_ref[rows, base:base + HEAD_DIM] = kb
        kf_ref[rows, base + HEAD_DIM:base + 2 * HEAD_DIM] = aug_k.astype(BF16)
        for name, val in (("q", qb), ("k", kb)):
            v32 = val.astype(F32)
            sq_max[(name, hd, half)] = jnp.max(jnp.sum(v32 * v32, axis=-1, keepdims=True),
                                               axis=0, keepdims=True)

    sq_max = {}
    m_steps = ([functools.partial(vft_tile, t) for t in range(W_MIX // NT)]
               + [functools.partial(plain, ga_ref, _C_GA, t) for t in range(d_model // NT)]
               + [functools.partial(plain, gf_ref, _C_GA + d_model, t) for t in range(d_model // NT)])
    v_steps = [functools.partial(qk_epilogue, hd, half) for hd in range(N_HEADS) for half in range(2)]
    for n, m_step in enumerate(m_steps):
        m_step()
        if n < len(v_steps):
            v_steps[n]()

    lane1 = lax.broadcasted_iota(jnp.int32, (1, LANES), 1)
    norm_rows = []
    for name in ("q", "k"):
        row = jnp.zeros((1, LANES), F32)
        for hd in range(N_HEADS):
            row = jnp.where(lane1 == hd, jnp.maximum(sq_max[(name, hd, 0)], sq_max[(name, hd, 1)]), row)
        norm_rows.append(row)
    stats_ref[...] = jnp.concatenate(
        [csum[0:1, :] * LOG2E, csum[tm - 1:tm, :] * LOG2E] + norm_rows
        + [jnp.zeros((SUBLANES - 4, LANES), F32)], axis=0)


def _inproj(x, shift, scale, g_norm, w_perm, conv_w, g_q, g_k, bf_lane, tm):
    B, T, D = x.shape
    n_w = w_perm.shape[1]
    tri = jnp.tril(jnp.ones((tm, tm), BF16))
    tok = lambda width: pl.BlockSpec((None, tm, width), lambda b, i: (b, i, 0))
    per_b = pl.BlockSpec((None, 1, D), lambda b, i: (b, 0, 0))
    const = lambda shape: pl.BlockSpec(shape, lambda b, i: (0,) * len(shape))
    bf16_out = lambda w: (tok(w), jax.ShapeDtypeStruct((B, T, w), BF16))
    outs = [bf16_out(3 * W_MIX), bf16_out(W_MIX), bf16_out(2 * W_MIX), bf16_out(2 * W_MIX),
            (pl.BlockSpec((None, None, W_MIX, tm), lambda b, i: (b, i, 0, 0)),
             jax.ShapeDtypeStruct((B, T // tm, W_MIX, tm), BF16)),
            bf16_out(W_MIX), bf16_out(D), bf16_out(D),
            (tok(LANES), jax.ShapeDtypeStruct((B, T, LANES), F32)),
            (pl.BlockSpec((None, None, SUBLANES, LANES), lambda b, i: (b, i, 0, 0)),
             jax.ShapeDtypeStruct((B, T // tm, SUBLANES, LANES), F32))]
    return pl.pallas_call(
        functools.partial(_inproj_kernel, tm=tm, d_model=D),
        grid=(B, T // tm),
        in_specs=[tok(D), per_b, per_b, const((1, D)), const((D, n_w)),
                  const((CONV_WIDTH, 3 * W_MIX)), const((tm, tm)),
                  const((1, HEAD_DIM)), const((1, HEAD_DIM)), const((1, LANES))],
        out_specs=[o[0] for o in outs],
        out_shape=[o[1] for o in outs],
        scratch_shapes=([pltpu.VMEM((1, LANES), F32)] + [pltpu.VMEM((tm + SUBLANES, W_MIX), F32)] * 3
                        + [pltpu.VMEM((D // LANES, tm, LANES), F32), pltpu.VMEM((tm, LANES), F32)]),
        compiler_params=pltpu.CompilerParams(dimension_semantics=("parallel", "arbitrary"),
                                             vmem_limit_bytes=VMEM_LIMIT),
        name="inproj",
    )(x, shift, scale, g_norm, w_perm, conv_w, tri, g_q, g_k, bf_lane)


def _gdn_masks():
    def token(n):
        within = n % GDN_CHUNK
        return n - within + SUBLANES * (within % SUBLANES) + within // SUBLANES

    row = lax.broadcasted_iota(jnp.int32, (GDN_BLOCK, GDN_BLOCK), 0)
    col_ = lax.broadcasted_iota(jnp.int32, (GDN_BLOCK, GDN_BLOCK), 1)
    r, c = token(row), token(col_)
    same = (r // GDN_CHUNK) == (c // GDN_CHUNK)
    lower = jnp.logical_and(same, r >= c)
    f_lower = jnp.where(lower, 1.0, 0.0)
    f_same = jnp.where(same, 1.0, 0.0)
    f_cum = jnp.concatenate([f_lower, f_same], axis=0).astype(BF16)
    f_upper = jnp.where(jnp.logical_and(same, r <= c), 1.0, 0.0).astype(BF16)
    eye = jnp.where(r == c, 1.0, 0.0).astype(F32)
    nm_lower = jnp.where(lower, 0.0, -MASK_BIG).astype(F32)
    nm_strict = jnp.where(jnp.logical_and(same, r > c), 0.0, -MASK_BIG).astype(F32)
    return f_cum, f_upper, eye, nm_lower, nm_strict


def _gdn_kernel(qkv_ref, slab_ref, gt_ref, acol_ref, dcol_ref, arow_ref, drow_ref,
                fcum_ref, fupper_ref, eye_ref, nml_ref, nms_ref,
                za_ref, gout_ref, o_ref, s_ref, obuf, *, tt):
    i = pl.program_id(1)
    C = GDN_CHUNK
    BLK = GDN_BLOCK
    n_chunks = tt // C
    shift = C.bit_length() - 1

    @pl.when(i == 0)
    def _():
        s_ref[...] = jnp.zeros_like(s_ref)

    f_cum = fcum_ref[...]
    f_upper = fupper_ref[...]
    eye = eye_ref[...]
    nm_lower = nml_ref[...]
    nm_strict = nms_ref[...]
    n_blk = tt // BLK
    per_blk = BLK // C
    n_fac = shift - 1
    assert per_blk == N_HEADS

    def split2(v):
        hi = v.astype(BF16)
        return hi, (v - hi.astype(F32)).astype(BF16)

    def col(slab_, lane):
        return jnp.broadcast_to(slab_[:, lane:lane + 1], (slab_.shape[0], HEAD_DIM))

    gates = {}

    def prep_gates(sb):
        rows = slice(sb * BLK, (sb + 1) * BLK)
        slab = slab_ref[rows, :]
        g_col = -jnp.exp(acol_ref[...]) * _softplus(slab + dcol_ref[...])
        g_row = -jnp.exp(arow_ref[...]) * _softplus(gt_ref[:, rows] + drow_ref[...])
        g_hi, g_lo = split2(g_col)
        cum = jnp.dot(f_cum, jnp.concatenate([g_hi, g_lo], axis=1), preferred_element_type=F32)
        cum = cum[:, :LANES] + cum[:, LANES:]
        gc_col, gl_col = cum[:BLK], cum[BLK:]
        r_hi, r_lo = split2(g_row)
        cr = jnp.dot(jnp.concatenate([r_hi, r_lo], axis=0), f_upper, preferred_element_type=F32)
        gates[sb] = dict(beta=_sigmoid(slab), gc_col=gc_col, gc_row=cr[:SUBLANES] + cr[SUBLANES:],
                         e_gc=jnp.exp(gc_col), e_rem=jnp.exp(gl_col - gc_col), e_tot=jnp.exp(gl_col))

    qk_m, rhs_uw, qg_m, kd_m, etot_m, p_m, x_m, u_m, w_m = {}, {}, {}, {}, {}, {}, {}, {}, {}

    def prep_head(sb, hd):
        pr = (sb, hd)
        gt = gates[sb]
        rows = slice(sb * BLK, (sb + 1) * BLK)
        qh = qkv_ref[rows, hd * HEAD_DIM:(hd + 1) * HEAD_DIM].astype(F32)
        kh = qkv_ref[rows, W_MIX + hd * HEAD_DIM:W_MIX + (hd + 1) * HEAD_DIM].astype(F32)
        vh = qkv_ref[rows, 2 * W_MIX + hd * HEAD_DIM:2 * W_MIX + (hd + 1) * HEAD_DIM].astype(F32)
        qn = qh * lax.rsqrt(jnp.sum(qh * qh, axis=-1, keepdims=True) + EPS) * (HEAD_DIM ** -0.5)
        kn = kh * lax.rsqrt(jnp.sum(kh * kh, axis=-1, keepdims=True) + EPS)
        bh = col(gt["beta"], _L_B + hd)
        e_gc = col(gt["e_gc"], _L_A + hd)
        kb = kn * bh
        rhs_uw[pr] = jnp.concatenate([vh * bh, kb * e_gc], axis=1).astype(BF16)
        qg_m[pr] = (qn * e_gc).astype(BF16)
        kd_m[pr] = (kn * col(gt["e_rem"], _L_A + hd)).astype(BF16)
        etot_m[pr] = col(gt["e_tot"], _L_A + hd)
        kq = jnp.dot(jnp.concatenate([kb, qn], axis=0).astype(BF16), kn.T.astype(BF16),
                     preferred_element_type=F32)
        diff = gt["gc_col"][:, _L_A + hd:_L_A + hd + 1] - gt["gc_row"][hd:hd + 1, :]
        a = kq[:BLK] * jnp.exp(diff + nm_strict)
        qk_m[pr] = (kq[BLK:] * jnp.exp(diff + nm_lower)).astype(BF16)
        ab = a.astype(BF16)
        p_m[pr] = eye - a
        x_m[pr] = jnp.dot(ab, ab, preferred_element_type=F32)

    def inv_iter(sb, j):
        for hd in range(N_HEADS):
            pr = (sb, hd)
            xb = x_m[pr].astype(BF16)
            if j + 1 < n_fac:
                px = jnp.dot(jnp.concatenate([p_m[pr].astype(BF16), xb], axis=0), xb,
                             preferred_element_type=F32)
                p_m[pr] = p_m[pr] + px[:BLK]
                x_m[pr] = px[BLK:]
            else:
                p_m[pr] = p_m[pr] + jnp.dot(p_m[pr].astype(BF16), xb, preferred_element_type=F32)

    def apply_inverse(sb):
        for hd in range(N_HEADS):
            pr = (sb, hd)
            uw = jnp.dot(p_m[pr].astype(BF16), rhs_uw[pr], preferred_element_type=F32)
            u_m[pr] = uw[:, :HEAD_DIM]
            w_m[pr] = uw[:, HEAD_DIM:].astype(BF16)

    state = [s_ref[hd] for hd in range(N_HEADS)]
    vnew_parts, qs_parts = {}, {}

    def state_chunk(sb, cb):
        rows = slice(cb * C, (cb + 1) * C)
        for hd in range(N_HEADS):
            pr = (sb, hd)
            rr = jnp.dot(jnp.concatenate([w_m[pr][rows], qg_m[pr][rows]], axis=0),
                         state[hd].astype(BF16), preferred_element_type=F32)
            vnew = u_m[pr][rows] - rr[:C]
            qs_parts.setdefault(pr, []).append(rr[C:])
            vnew_parts.setdefault(pr, []).append(vnew)
            upd = lax.dot_general(kd_m[pr][rows], vnew.astype(BF16), (((0,), (0,)), ((), ())),
                                  preferred_element_type=F32)
            state[hd] = state[hd] * etot_m[pr][cb * C:cb * C + 1, :] + upd

    def out_head(sb, hd):
        pr = (sb, hd)
        rows = slice(sb * BLK, (sb + 1) * BLK)
        sl = slice(hd * HEAD_DIM, (hd + 1) * HEAD_DIM)
        vn = jnp.concatenate(vnew_parts[pr], axis=0).astype(BF16)
        o = jnp.concatenate(qs_parts[pr], axis=0) + jnp.dot(qk_m[pr], vn, preferred_element_type=F32)
        for g in range(BLK // SUBLANES):
            ch, j = divmod(g, C // SUBLANES)
            obuf[hd, pl.ds(ch * C + j, SUBLANES, stride=SUBLANES), :] = o[g * SUBLANES:(g + 1) * SUBLANES, :]
        o = obuf[hd]
        on = o * lax.rsqrt(jnp.mean(o * o, axis=-1, keepdims=True) + EPS) * gout_ref[...]
        o_ref[rows, sl] = (on * _silu(za_ref[rows, sl].astype(F32))).astype(BF16)

    inv_share = [[] for _ in range(per_blk)]
    for j in range(n_fac):
        inv_share[max(0, j - (n_fac - per_blk))].append(j)
    for t in range(-2, n_blk + 1):
        if 0 <= t + 2 < n_blk:
            prep_gates(t + 2)
        for slot in range(per_blk):
            if 0 <= t < n_blk:
                state_chunk(t, slot)
            if 0 <= t + 1 < n_blk:
                for j in inv_share[slot]:
                    inv_iter(t + 1, j)
                if slot == per_blk - 1:
                    apply_inverse(t + 1)
            if 0 <= t + 2 < n_blk:
                prep_head(t + 2, slot)
            if 0 <= t - 1 < n_blk:
                out_head(t - 1, slot)
    for hd in range(N_HEADS):
        s_ref[hd] = state[hd]


def _gdn(qkva, slab, gates_t, a_log, dt_bias, za, g_out, tt):
    B, T, _ = qkva.shape
    tok = lambda width: pl.BlockSpec((None, tt, width), lambda b, i: (b, i, 0))
    const = lambda shape: pl.BlockSpec(shape, lambda b, i: (0,) * len(shape))
    lane_vec = lambda v: jnp.zeros((1, LANES), F32).at[0, _L_A:_L_A + N_HEADS].set(v)
    sub_vec = lambda v: jnp.zeros((SUBLANES, 1), F32).at[0:N_HEADS, 0].set(v)
    masks = _gdn_masks()
    return pl.pallas_call(
        functools.partial(_gdn_kernel, tt=tt),
        grid=(B, T // tt),
        in_specs=[tok(3 * W_MIX), tok(LANES),
                  pl.BlockSpec((None, SUBLANES, tt), lambda b, i: (b, 0, i)),
                  const((1, LANES)), const((1, LANES)), const((SUBLANES, 1)), const((SUBLANES, 1))]
                 + [const(m.shape) for m in masks]
                 + [tok(W_MIX), const((1, HEAD_DIM))],
        out_specs=tok(W_MIX),
        out_shape=jax.ShapeDtypeStruct((B, T, W_MIX), BF16),
        scratch_shapes=[pltpu.VMEM((N_HEADS, HEAD_DIM, HEAD_DIM), F32),
                        pltpu.VMEM((N_HEADS, GDN_BLOCK, HEAD_DIM), F32)],
        compiler_params=pltpu.CompilerParams(dimension_semantics=("parallel", "arbitrary"),
                                             vmem_limit_bytes=VMEM_LIMIT),
        name="gdn",
    )(qkva, slab, gates_t, lane_vec(a_log), lane_vec(dt_bias), sub_vec(a_log), sub_vec(dt_bias),
      *masks, za, g_out.reshape(1, HEAD_DIM))


def _fox_kernel(fp_ref, q_ref, k_ref, vt_ref, zf_ref, o_ref, m_ref, acc_ref, sa_ref, sb_ref, qt_ref,
                *, tq, tk, n_q):
    qi = pl.program_id(2)
    p0 = fp_ref[(pl.program_id(0) * N_HEADS + pl.program_id(1)) * n_q + qi]
    m_ref[...] = jnp.full_like(m_ref, NEG_INF)
    acc_ref[...] = jnp.zeros_like(acc_ref)
    qt_ref[...] = q_ref[...].T
    ones_rows = jnp.ones((FOX_SUM_ROWS, tk), BF16)
    halves = (slice(0, tk), slice(tk, 2 * tk))

    def scores_into(buf, kb, which=(0, 1)):
        k = k_ref[pl.ds(pl.multiple_of(kb * tk, tk), tk), :]
        for h in which:
            buf[h] = jnp.dot(k, qt_ref[:, halves[h]], preferred_element_type=F32)

    def consume(buf, kb, modes):
        vt = jnp.concatenate([vt_ref[kb], ones_rows], axis=0)
        live = [h for h in (0, 1) if modes[h] != "skip"]
        pts, alphas = {}, {}
        for h in live:
            st = buf[h]
            if modes[h] == "diag":
                krow = lax.broadcasted_iota(jnp.int32, (tk, tk), 0)
                qcol = lax.broadcasted_iota(jnp.int32, (tk, tk), 1)
                st = jnp.where(qcol >= krow, st, NEG_INF)
            m_prev = m_ref[:, halves[h]]
            m_new = jnp.maximum(m_prev, jnp.max(st, axis=0, keepdims=True))
            alphas[h] = jnp.exp2(m_prev - m_new)
            pts[h] = jnp.exp2(st - m_new).astype(BF16)
            m_ref[:, halves[h]] = m_new
        for h in live:
            acc_ref[:, halves[h]] = (alphas[h] * acc_ref[:, halves[h]]
                                     + jnp.dot(vt, pts[h], preferred_element_type=F32))

    full = ("full", "full")
    scores_into(sa_ref, 2 * p0)

    def pair(p, carry):
        kb = 2 * p
        scores_into(sb_ref, kb + 1)
        consume(sa_ref, kb, full)
        scores_into(sa_ref, kb + 2)
        consume(sb_ref, kb + 1, full)
        return carry

    lax.fori_loop(p0, qi, pair, 0)
    scores_into(sb_ref, 2 * qi + 1, which=(1,))
    consume(sa_ref, 2 * qi, ("diag", "full"))
    consume(sb_ref, 2 * qi + 1, ("skip", "diag"))

    ot = acc_ref[0:HEAD_DIM, :] * (1.0 / acc_ref[HEAD_DIM:HEAD_DIM + 1, :])
    o_ref[...] = (ot.T * _silu(zf_ref[...].astype(F32))).astype(BF16)


def _fox_first_pair(stats, tk, n_q):
    c_first = stats[:, :, 0, _L_F:_L_F + N_HEADS]
    c_last = stats[:, :, 1, _L_F:_L_F + N_HEADS]
    qk_bound = jnp.sqrt(jnp.max(stats[:, :, 2, :N_HEADS], axis=1) * jnp.max(stats[:, :, 3, :N_HEADS], axis=1))
    n_kb = stats.shape[1]
    upper = (2.0 * qk_bound[:, None, None, :] + c_first[:, 0::2, None, :][:, :n_q]
             - c_last[:, None, :, :])
    kb = jnp.arange(n_kb, dtype=jnp.int32)[None, None, :, None]
    diag = 2 * jnp.arange(n_q, dtype=jnp.int32)[None, :, None, None]
    needed = jnp.logical_or(upper >= -FOX_SKIP_LOG2, kb >= diag)
    first_block = jnp.min(jnp.where(needed, kb, n_kb), axis=2)
    return jnp.right_shift(first_block, 1).transpose(0, 2, 1).reshape(-1)


def _fox(qf_aug, kf_aug, vft, zf, stats, tk):
    B, T, _ = zf.shape
    tq = 2 * tk
    assert T % tq == 0
    n_q = T // tq
    first_pair = _fox_first_pair(stats, tk, n_q)
    grid_spec = pltpu.PrefetchScalarGridSpec(
        num_scalar_prefetch=1,
        grid=(B, N_HEADS, n_q),
        in_specs=[pl.BlockSpec((None, tq, 2 * HEAD_DIM), lambda b, h, i, fp: (b, i, h)),
                  pl.BlockSpec((None, T, 2 * HEAD_DIM), lambda b, h, i, fp: (b, 0, h)),
                  pl.BlockSpec((None, T // tk, HEAD_DIM, tk), lambda b, h, i, fp: (b, 0, h, 0)),
                  pl.BlockSpec((None, tq, HEAD_DIM), lambda b, h, i, fp: (b, i, h))],
        out_specs=pl.BlockSpec((None, tq, HEAD_DIM), lambda b, h, i, fp: (b, i, h)),
        scratch_shapes=[pltpu.VMEM((1, tq), F32), pltpu.VMEM((HEAD_DIM + FOX_SUM_ROWS, tq), F32),
                        pltpu.VMEM((2, tk, tk), F32), pltpu.VMEM((2, tk, tk), F32),
                        pltpu.VMEM((2 * HEAD_DIM, tq), BF16)])
    return pl.pallas_call(
        functools.partial(_fox_kernel, tq=tq, tk=tk, n_q=n_q),
        grid_spec=grid_spec,
        out_shape=jax.ShapeDtypeStruct((B, T, W_MIX), BF16),
        compiler_params=pltpu.CompilerParams(dimension_semantics=("parallel", "parallel", "arbitrary"),
                                             vmem_limit_bytes=VMEM_LIMIT),
        name="fox",
    )(first_pair, qf_aug, kf_aug, vft, zf)


def _outproj_kernel(oa_ref, of_ref, ga_ref, gf_ref, x_ref, gate_ref, wa_ref, wf_ref, wo_ref, o_ref):
    ya = jnp.dot(oa_ref[...], wa_ref[...], preferred_element_type=F32)
    yf = jnp.dot(of_ref[...], wf_ref[...], preferred_element_type=F32)
    merged = _sigmoid(ga_ref[...].astype(F32)) * ya + _sigmoid(gf_ref[...].astype(F32)) * yf
    res = jnp.dot(merged.astype(BF16), wo_ref[...], preferred_element_type=F32)
    o_ref[...] = x_ref[...] + gate_ref[...] * res


def _outproj(oa, of, ga, gf, x, gate, wa, wf, wo, tm):
    B, T, D = x.shape
    tok = lambda width: pl.BlockSpec((None, tm, width), lambda b, i: (b, i, 0))
    const = lambda shape: pl.BlockSpec(shape, lambda b, i: (0,) * len(shape))
    return pl.pallas_call(
        _outproj_kernel,
        grid=(B, T // tm),
        in_specs=[tok(W_MIX), tok(W_MIX), tok(D), tok(D), tok(D),
                  pl.BlockSpec((None, 1, D), lambda b, i: (b, 0, 0)),
                  const((W_MIX, D)), const((W_MIX, D)), const((D, D))],
        out_specs=tok(D),
        out_shape=jax.ShapeDtypeStruct((B, T, D), F32),
        compiler_params=pltpu.CompilerParams(dimension_semantics=("parallel", "parallel"),
                                             vmem_limit_bytes=VMEM_LIMIT),
        name="outproj",
    )(oa, of, ga, gf, x, gate, wa, wf, wo)


def _permute_w_in(w_in, d_model):
    o = 0
    cols = {}
    for name, width in (("qkva", 3 * W_MIX), ("a", N_HEADS), ("b", N_HEADS), ("za", W_MIX),
                        ("qf", W_MIX), ("kf", W_MIX), ("vf", W_MIX), ("f", N_HEADS), ("zf", W_MIX),
                        ("ga", d_model), ("gf", d_model)):
        cols[name] = w_in[:, o:o + width]
        o += width
    gates = jnp.concatenate([cols["a"], cols["b"], cols["f"],
                             jnp.zeros((w_in.shape[0], LANES - 3 * N_HEADS), w_in.dtype)], axis=1)
    return jnp.concatenate([cols["qkva"], cols["za"], cols["qf"], cols["kf"], cols["vf"], cols["zf"],
                            cols["ga"], cols["gf"], gates], axis=1).astype(BF16)


def kernel(x, c, w_ada, b_ada, g_norm, w_in, conv_w, A_log, dt_bias, g_gdn_out,
           g_q_fox, g_k_fox, b_f, w_o_gdn, w_o_fox, w_out):
    B, T, D = x.shape
    tm = tk = min(512, T)
    tt = min(4 * GDN_BLOCK, T)

    mod = _adaln(c, w_ada, b_ada)
    shift, scale, gate = (mod[:, j * D:(j + 1) * D].reshape(B, 1, D) for j in range(3))

    w_perm = _permute_w_in(w_in, D)
    bf_lane = jnp.zeros((1, LANES), F32).at[0, _L_F:_L_F + N_HEADS].set(b_f)
    (qkva, za, qf_aug, kf_aug, vft, zf, ga, gf, slab, stats) = _inproj(
        x, shift, scale, g_norm.reshape(1, D), w_perm, conv_w,
        g_q_fox.reshape(1, HEAD_DIM), g_k_fox.reshape(1, HEAD_DIM), bf_lane, tm)

    gates_t = jnp.transpose(slab[:, :, :SUBLANES], (0, 2, 1))
    oa = _gdn(qkva, slab, gates_t, A_log, dt_bias, za, g_gdn_out, tt)
    of = _fox(qf_aug, kf_aug, vft, zf, stats, tk)
    return _outproj(oa, of, ga, gf, x, gate, w_o_gdn.astype(BF16), w_o_fox.astype(BF16),
                    w_out.astype(BF16), tm)
```

```python
import functools

import jax
import jax.numpy as jnp
from jax import lax
from jax.experimental import pallas as pl
from jax.experimental.pallas import tpu as pltpu

F32 = jnp.float32
BF16 = jnp.bfloat16
EPS = 1e-6
HEAD_DIM = 128
N_HEADS = 4
W_MIX = N_HEADS * HEAD_DIM
CONV_WIDTH = 4
GDN_CHUNK = 64
GDN_BLOCK = 256
MASK_BIG = 1e30
LANES = 128
MXU_TILE = 256
SUBLANES = 8
VMEM_LIMIT = 56 * 1024 * 1024
HI = lax.Precision.HIGHEST
NEG_INF = float("-inf")
LOG2E = 1.4426950408889634
FOX_SUM_ROWS = 16
FOX_SKIP_LOG2 = 152.0

_C_QKVA = 0
_C_ZA = 3 * W_MIX
_C_QF = _C_ZA + W_MIX
_C_KF = _C_QF + W_MIX
_C_VF = _C_KF + W_MIX
_C_ZF = _C_VF + W_MIX
_C_GA = _C_ZF + W_MIX
_L_A, _L_B, _L_F = 0, N_HEADS, 2 * N_HEADS


def _softplus(z):
    return jnp.maximum(z, 0.0) + jnp.log(1.0 + jnp.exp(-jnp.abs(z)))


def _log_sigmoid(z):
    return jnp.minimum(z, 0.0) - jnp.log(1.0 + jnp.exp(-jnp.abs(z)))


def _sigmoid(z):
    return 1.0 / (1.0 + jnp.exp(-z))


def _silu(z):
    return z * _sigmoid(z)


def _split3_bf16(c):
    hi = c.astype(BF16).astype(F32)
    r = c - hi
    mid = r.astype(BF16).astype(F32)
    lo = (r - mid).astype(BF16).astype(F32)
    return hi, mid, lo


def _adaln_kernel(c_ref, w_ref, b_ref, o_ref):
    o_ref[...] = jnp.dot(c_ref[...], w_ref[...], precision=HI, preferred_element_type=F32) + b_ref[...]


def _adaln(c, w_ada, b_ada):
    B, D = c.shape
    n = w_ada.shape[1]
    bn = D
    return pl.pallas_call(
        _adaln_kernel,
        grid=(n // bn,),
        in_specs=[pl.BlockSpec((B, D), lambda j: (0, 0)),
                  pl.BlockSpec((D, bn), lambda j: (0, j)),
                  pl.BlockSpec((1, bn), lambda j: (0, j))],
        out_specs=pl.BlockSpec((B, bn), lambda j: (0, j)),
        out_shape=jax.ShapeDtypeStruct((B, n), F32),
        compiler_params=pltpu.CompilerParams(dimension_semantics=("arbitrary",), vmem_limit_bytes=VMEM_LIMIT),
        name="adaln",
    )(c, w_ada, b_ada.reshape(1, n))


def _inproj_kernel(x_ref, shift_ref, scale_ref, gn_ref, w_ref, convw_ref, tri_ref,
                   gq_ref, gk_ref, bf_ref,
                   qkva_ref, za_ref, qf_ref, kf_ref, vft_ref, zf_ref, ga_ref, gf_ref, gates_ref, stats_ref,
                   carry_ref, xbuf_q, xbuf_k, xbuf_v, hbuf, gbuf, *, tm, d_model):
    i = pl.program_id(1)
    xbufs = (xbuf_q, xbuf_k, xbuf_v)

    @pl.when(i == 0)
    def _():
        carry_ref[...] = jnp.zeros_like(carry_ref)
        for xb in xbufs:
            xb[0:SUBLANES, :] = jnp.zeros((SUBLANES, W_MIX), F32)

    x = x_ref[...]
    ms = jnp.mean(x * x, axis=-1, keepdims=True)
    h = x * lax.rsqrt(ms + EPS) * gn_ref[...]
    h = h * (1.0 + scale_ref[...]) + shift_ref[...]
    hb = h.astype(BF16)

    def chunk_permuted(ref):
        return jnp.concatenate([ref[pl.ds(c * GDN_CHUNK + j, SUBLANES, stride=SUBLANES), :]
                                for c in range(tm // GDN_CHUNK) for j in range(SUBLANES)], axis=0)

    for kg in range(d_model // LANES):
        hbuf[kg] = h[:, kg * LANES:(kg + 1) * LANES]
    hbp = jnp.concatenate([chunk_permuted(hbuf.at[kg]) for kg in range(d_model // LANES)],
                          axis=1).astype(BF16)

    def proj(lo, width):
        return jnp.dot(hb, w_ref[:, lo:lo + width], preferred_element_type=F32)

    NT = MXU_TILE
    RC = tm // 4

    def qkva_tile(t):
        j, half = divmod(t, W_MIX // NT)
        xbufs[j][SUBLANES:SUBLANES + tm, half * NT:(half + 1) * NT] = jnp.dot(
            hbp, w_ref[:, _C_QKVA + t * NT:_C_QKVA + (t + 1) * NT], preferred_element_type=F32)

    def conv_silu(j, rc):
        xbuf = xbufs[j]
        cols = slice(j * W_MIX, (j + 1) * W_MIX)
        n_tail = CONV_WIDTH - 1
        for c in range(rc * RC // GDN_CHUNK, (rc + 1) * RC // GDN_CHUNK):
            b0 = SUBLANES + c * GDN_CHUNK
            wrapped = []
            for g in range(SUBLANES - n_tail, SUBLANES):
                prev = g if c == 0 else b0 - GDN_CHUNK + SUBLANES * g + SUBLANES - 1
                wrapped += [xbuf[prev:prev + 1, :], xbuf[b0 + SUBLANES * g:b0 + SUBLANES * (g + 1) - 1, :]]
            ext = jnp.concatenate(wrapped + [xbuf[b0:b0 + GDN_CHUNK, :]], axis=0)
            y = ext[0:GDN_CHUNK] * convw_ref[0:1, cols]
            for tap in range(1, CONV_WIDTH):
                y = y + ext[SUBLANES * tap:SUBLANES * tap + GDN_CHUNK] * convw_ref[tap:tap + 1, cols]
            qkva_ref[c * GDN_CHUNK:(c + 1) * GDN_CHUNK, cols] = _silu(y).astype(BF16)
        if rc == tm // RC - 1:
            last = SUBLANES + tm - GDN_CHUNK
            for g in range(SUBLANES - n_tail, SUBLANES):
                r = last + SUBLANES * g + SUBLANES - 1
                xbuf[g:g + 1, :] = xbuf[r:r + 1, :]

    def plain(out_ref, lo, t):
        out_ref[:, t * NT:(t + 1) * NT] = proj(lo + t * NT, NT).astype(BF16)

    def vft_tile(t):
        vft_ref[t * NT:(t + 1) * NT, :] = proj(_C_VF + t * NT, NT).astype(BF16).T

    qkva_tile(0)
    qkva_tile(1)
    for t in (2, 3, 4, 5):
        qkva_tile(t)
        conv_silu(0, t - 2)
    g = proj(_C_GA + 2 * d_model, LANES)
    conv_silu(1, 0)
    qk_tiles = {}
    for n, (name, lo) in enumerate((("q", _C_QF), ("q", _C_QF + NT), ("k", _C_KF), ("k", _C_KF + NT))):
        qk_tiles[(name, n % 2)] = proj(lo, NT)
        if n < 3:
            conv_silu(1, n + 1)

    gbuf[...] = g
    gates_ref[...] = chunk_permuted(gbuf)
    logf = _log_sigmoid(g + bf_ref[...])
    hi, mid, lo = _split3_bf16(logf)
    lane_g = lax.broadcasted_iota(jnp.int32, (tm, LANES), 1)
    packed = jnp.where(lane_g < _L_F + N_HEADS, hi,
                       jnp.where(lane_g < _L_F + 2 * N_HEADS, pltpu.roll(mid, N_HEADS, 1),
                                 jnp.where(lane_g < _L_F + 3 * N_HEADS, pltpu.roll(lo, 2 * N_HEADS, 1), 0.0)))
    cs = jnp.dot(tri_ref[...], packed.astype(BF16), preferred_element_type=F32)
    csum = (cs + pltpu.roll(cs, LANES - N_HEADS, 1)) + pltpu.roll(cs, LANES - 2 * N_HEADS, 1) + carry_ref[...]
    carry_ref[...] = csum[tm - 1:tm, :]

    for n, (ref, lo, t) in enumerate(((za_ref, _C_ZA, 0), (za_ref, _C_ZA, 1),
                                      (zf_ref, _C_ZF, 0), (zf_ref, _C_ZF, 1))):
        plain(ref, lo, t)
        conv_silu(2, n)

    HR = tm // 2
    lane = lax.broadcasted_iota(jnp.int32, (HR, LANES), 1)
    one = jnp.where(lane < 3, 0.0, jnp.where(lane < 6, 1.0, 0.0))
    one_k = jnp.where(lane < 3, 1.0, 0.0)

    def qk_epilogue(hd, half):
        rows = slice(half * HR, (half + 1) * HR)
        sl = slice((hd % 2) * HEAD_DIM, (hd % 2 + 1) * HEAD_DIM)
        qh = qk_tiles[("q", hd // 2)][rows, sl]
        kh = qk_tiles[("k", hd // 2)][rows, sl]
        qn = qh * lax.rsqrt(jnp.mean(qh * qh, axis=-1, keepdims=True) + EPS) * gq_ref[...]
        kn = kh * lax.rsqrt(jnp.mean(kh * kh, axis=-1, keepdims=True) + EPS) * gk_ref[...]
        ch = jnp.broadcast_to(csum[rows, _L_F + hd:_L_F + hd + 1], (HR, LANES)) * LOG2E
        hi, mid, lo = _split3_bf16(ch)
        aug_q = jnp.where(lane == 0, hi, jnp.where(lane == 1, mid, jnp.where(lane == 2, lo, one)))
        aug_k = jnp.where(lane == 3, -hi, jnp.where(lane == 4, -mid, jnp.where(lane == 5, -lo, one_k)))
        base = hd * 2 * HEAD_DIM
        qb = (qn * (HEAD_DIM ** -0.5 * LOG2E)).astype(BF16)
        kb = kn.astype(BF16)
        qf_ref[base:base + HEAD_DIM, rows] = qb.T
        qf_ref[base + HEAD_DIM:base + 2 * HEAD_DIM, rows] = aug_q.astype(BF16).T
        kf_ref[rows, base:base + HEAD_DIM] = kb
        kf_ref[rows, base + HEAD_DIM:base + 2 * HEAD_DIM] = aug_k.astype(BF16)
        for name, val in (("q", qb), ("k", kb)):
            v32 = val.astype(F32)
            sq_max[(name, hd, half)] = jnp.max(jnp.sum(v32 * v32, axis=-1, keepdims=True),
                                               axis=0, keepdims=True)

    sq_max = {}
    m_steps = ([functools.partial(vft_tile, t) for t in range(W_MIX // NT)]
               + [functools.partial(plain, ga_ref, _C_GA, t) for t in range(d_model // NT)]
               + [functools.partial(plain, gf_ref, _C_GA + d_model, t) for t in range(d_model // NT)])
    v_steps = [functools.partial(qk_epilogue, hd, half) for hd in range(N_HEADS) for half in range(2)]
    for n, m_step in enumerate(m_steps):
        m_step()
        if n < len(v_steps):
            v_steps[n]()

    lane1 = lax.broadcasted_iota(jnp.int32, (1, LANES), 1)
    norm_rows = []
    for name in ("q", "k"):
        row = jnp.zeros((1, LANES), F32)
        for hd in range(N_HEADS):
            row = jnp.where(lane1 == hd, jnp.maximum(sq_max[(name, hd, 0)], sq_max[(name, hd, 1)]), row)
        norm_rows.append(row)
    stats_ref[...] = jnp.concatenate(
        [csum[0:1, :] * LOG2E, csum[tm - 1:tm, :] * LOG2E] + norm_rows
        + [jnp.zeros((SUBLANES - 4, LANES), F32)], axis=0)


def _inproj(x, shift, scale, g_norm, w_perm, conv_w, g_q, g_k, bf_lane, tm):
    B, T, D = x.shape
    n_w = w_perm.shape[1]
    tri = jnp.tril(jnp.ones((tm, tm), BF16))
    tok = lambda width: pl.BlockSpec((None, tm, width), lambda b, i: (b, i, 0))
    per_b = pl.BlockSpec((None, 1, D), lambda b, i: (b, 0, 0))
    const = lambda shape: pl.BlockSpec(shape, lambda b, i: (0,) * len(shape))
    bf16_out = lambda w: (tok(w), jax.ShapeDtypeStruct((B, T, w), BF16))
    feat_major = lambda w: (pl.BlockSpec((None, None, w, tm), lambda b, i: (b, i, 0, 0)),
                            jax.ShapeDtypeStruct((B, T // tm, w, tm), BF16))
    outs = [bf16_out(3 * W_MIX), bf16_out(W_MIX), feat_major(2 * W_MIX), bf16_out(2 * W_MIX),
            feat_major(W_MIX),
            bf16_out(W_MIX), bf16_out(D), bf16_out(D),
            (tok(LANES), jax.ShapeDtypeStruct((B, T, LANES), F32)),
            (pl.BlockSpec((None, None, SUBLANES, LANES), lambda b, i: (b, i, 0, 0)),
             jax.ShapeDtypeStruct((B, T // tm, SUBLANES, LANES), F32))]
    return pl.pallas_call(
        functools.partial(_inproj_kernel, tm=tm, d_model=D),
        grid=(B, T // tm),
        in_specs=[tok(D), per_b, per_b, const((1, D)), const((D, n_w)),
                  const((CONV_WIDTH, 3 * W_MIX)), const((tm, tm)),
                  const((1, HEAD_DIM)), const((1, HEAD_DIM)), const((1, LANES))],
        out_specs=[o[0] for o in outs],
        out_shape=[o[1] for o in outs],
        scratch_shapes=([pltpu.VMEM((1, LANES), F32)] + [pltpu.VMEM((tm + SUBLANES, W_MIX), F32)] * 3
                        + [pltpu.VMEM((D // LANES, tm, LANES), F32), pltpu.VMEM((tm, LANES), F32)]),
        compiler_params=pltpu.CompilerParams(dimension_semantics=("parallel", "arbitrary"),
                                             vmem_limit_bytes=VMEM_LIMIT),
        name="inproj",
    )(x, shift, scale, g_norm, w_perm, conv_w, tri, g_q, g_k, bf_lane)


def _gdn_masks():
    def token(n):
        within = n % GDN_CHUNK
        return n - within + SUBLANES * (within % SUBLANES) + within // SUBLANES

    row = lax.broadcasted_iota(jnp.int32, (GDN_BLOCK, GDN_BLOCK), 0)
    col_ = lax.broadcasted_iota(jnp.int32, (GDN_BLOCK, GDN_BLOCK), 1)
    r, c = token(row), token(col_)
    same = (r // GDN_CHUNK) == (c // GDN_CHUNK)
    lower = jnp.logical_and(same, r >= c)
    f_lower = jnp.where(lower, 1.0, 0.0)
    f_same = jnp.where(same, 1.0, 0.0)
    f_cum = jnp.concatenate([f_lower, f_same], axis=0).astype(BF16)
    f_upper = jnp.where(jnp.logical_and(same, r <= c), 1.0, 0.0).astype(BF16)
    eye = jnp.where(r == c, 1.0, 0.0).astype(F32)
    nm_lower = jnp.where(lower, 0.0, -MASK_BIG).astype(F32)
    nm_strict = jnp.where(jnp.logical_and(same, r > c), 0.0, -MASK_BIG).astype(F32)
    return f_cum, f_upper, eye, nm_lower, nm_strict


def _gdn_kernel(qkv_ref, slab_ref, gt_ref, acol_ref, dcol_ref, arow_ref, drow_ref,
                fcum_ref, fupper_ref, eye_ref, nml_ref, nms_ref,
                za_ref, gout_ref, o_ref, s_ref, obuf, *, tt):
    i = pl.program_id(1)
    C = GDN_CHUNK
    BLK = GDN_BLOCK
    n_chunks = tt // C
    shift = C.bit_length() - 1

    @pl.when(i == 0)
    def _():
        s_ref[...] = jnp.zeros_like(s_ref)

    f_cum = fcum_ref[...]
    f_upper = fupper_ref[...]
    eye = eye_ref[...]
    nm_lower = nml_ref[...]
    nm_strict = nms_ref[...]
    n_blk = tt // BLK
    per_blk = BLK // C
    n_fac = shift - 1
    assert per_blk == N_HEADS

    def split2(v):
        hi = v.astype(BF16)
        return hi, (v - hi.astype(F32)).astype(BF16)

    def col(slab_, lane):
        return jnp.broadcast_to(slab_[:, lane:lane + 1], (slab_.shape[0], HEAD_DIM))

    gates = {}

    def prep_gates(sb):
        rows = slice(sb * BLK, (sb + 1) * BLK)
        slab = slab_ref[rows, :]
        g_col = -jnp.exp(acol_ref[...]) * _softplus(slab + dcol_ref[...])
        g_row = -jnp.exp(arow_ref[...]) * _softplus(gt_ref[:, rows] + drow_ref[...])
        g_hi, g_lo = split2(g_col)
        cum = jnp.dot(f_cum, jnp.concatenate([g_hi, g_lo], axis=1), preferred_element_type=F32)
        cum = cum[:, :LANES] + cum[:, LANES:]
        gc_col, gl_col = cum[:BLK], cum[BLK:]
        r_hi, r_lo = split2(g_row)
        cr = jnp.dot(jnp.concatenate([r_hi, r_lo], axis=0), f_upper, preferred_element_type=F32)
        gates[sb] = dict(beta=_sigmoid(slab), gc_col=gc_col, gc_row=cr[:SUBLANES] + cr[SUBLANES:],
                         e_gc=jnp.exp(gc_col), e_rem=jnp.exp(gl_col - gc_col), e_tot=jnp.exp(gl_col))

    qk_m, rhs_uw, qg_m, kd_m, etot_m, p_m, x_m, u_m, w_m = {}, {}, {}, {}, {}, {}, {}, {}, {}

    def prep_head(sb, hd):
        pr = (sb, hd)
        gt = gates[sb]
        rows = slice(sb * BLK, (sb + 1) * BLK)
        qh = qkv_ref[rows, hd * HEAD_DIM:(hd + 1) * HEAD_DIM].astype(F32)
        kh = qkv_ref[rows, W_MIX + hd * HEAD_DIM:W_MIX + (hd + 1) * HEAD_DIM].astype(F32)
        vh = qkv_ref[rows, 2 * W_MIX + hd * HEAD_DIM:2 * W_MIX + (hd + 1) * HEAD_DIM].astype(F32)
        qn = qh * lax.rsqrt(jnp.sum(qh * qh, axis=-1, keepdims=True) + EPS) * (HEAD_DIM ** -0.5)
        kn = kh * lax.rsqrt(jnp.sum(kh * kh, axis=-1, keepdims=True) + EPS)
        bh = col(gt["beta"], _L_B + hd)
        e_gc = col(gt["e_gc"], _L_A + hd)
        kb = kn * bh
        rhs_uw[pr] = jnp.concatenate([vh * bh, kb * e_gc], axis=1).astype(BF16)
        qg_m[pr] = (qn * e_gc).astype(BF16)
        kd_m[pr] = (kn * col(gt["e_rem"], _L_A + hd)).astype(BF16)
        etot_m[pr] = col(gt["e_tot"], _L_A + hd)
        kq = jnp.dot(jnp.concatenate([kb, qn], axis=0).astype(BF16), kn.T.astype(BF16),
                     preferred_element_type=F32)
        diff = gt["gc_col"][:, _L_A + hd:_L_A + hd + 1] - gt["gc_row"][hd:hd + 1, :]
        a = kq[:BLK] * jnp.exp(diff + nm_strict)
        qk_m[pr] = (kq[BLK:] * jnp.exp(diff + nm_lower)).astype(BF16)
        ab = a.astype(BF16)
        p_m[pr] = eye - a
        x_m[pr] = jnp.dot(ab, ab, preferred_element_type=F32)

    def inv_iter(sb, j):
        for hd in range(N_HEADS):
            pr = (sb, hd)
            xb = x_m[pr].astype(BF16)
            if j + 1 < n_fac:
                px = jnp.dot(jnp.concatenate([p_m[pr].astype(BF16), xb], axis=0), xb,
                             preferred_element_type=F32)
                p_m[pr] = p_m[pr] + px[:BLK]
                x_m[pr] = px[BLK:]
            else:
                p_m[pr] = p_m[pr] + jnp.dot(p_m[pr].astype(BF16), xb, preferred_element_type=F32)

    def apply_inverse(sb):
        for hd in range(N_HEADS):
            pr = (sb, hd)
            uw = jnp.dot(p_m[pr].astype(BF16), rhs_uw[pr], preferred_element_type=F32)
            u_m[pr] = uw[:, :HEAD_DIM]
            w_m[pr] = uw[:, HEAD_DIM:].astype(BF16)

    state = [s_ref[hd] for hd in range(N_HEADS)]
    vnew_parts, qs_parts = {}, {}

    def state_chunk(sb, cb):
        rows = slice(cb * C, (cb + 1) * C)
        for hd in range(N_HEADS):
            pr = (sb, hd)
            rr = jnp.dot(jnp.concatenate([w_m[pr][rows], qg_m[pr][rows]], axis=0),
                         state[hd].astype(BF16), preferred_element_type=F32)
            vnew = u_m[pr][rows] - rr[:C]
            qs_parts.setdefault(pr, []).append(rr[C:])
            vnew_parts.setdefault(pr, []).append(vnew)
            upd = lax.dot_general(kd_m[pr][rows], vnew.astype(BF16), (((0,), (0,)), ((), ())),
                                  preferred_element_type=F32)
            state[hd] = state[hd] * etot_m[pr][cb * C:cb * C + 1, :] + upd

    def out_head(sb, hd):
        pr = (sb, hd)
        rows = slice(sb * BLK, (sb + 1) * BLK)
        sl = slice(hd * HEAD_DIM, (hd + 1) * HEAD_DIM)
        vn = jnp.concatenate(vnew_parts[pr], axis=0).astype(BF16)
        o = jnp.concatenate(qs_parts[pr], axis=0) + jnp.dot(qk_m[pr], vn, preferred_element_type=F32)
        for g in range(BLK // SUBLANES):
            ch, j = divmod(g, C // SUBLANES)
            obuf[hd, pl.ds(ch * C + j, SUBLANES, stride=SUBLANES), :] = o[g * SUBLANES:(g + 1) * SUBLANES, :]
        o = obuf[hd]
        on = o * lax.rsqrt(jnp.mean(o * o, axis=-1, keepdims=True) + EPS) * gout_ref[...]
        o_ref[rows, sl] = (on * _silu(za_ref[rows, sl].astype(F32))).astype(BF16)

    inv_share = [[] for _ in range(per_blk)]
    for j in range(n_fac):
        inv_share[max(0, j - (n_fac - per_blk))].append(j)
    for t in range(-2, n_blk + 1):
        if 0 <= t + 2 < n_blk:
            prep_gates(t + 2)
        for slot in range(per_blk):
            if 0 <= t < n_blk:
                state_chunk(t, slot)
            if 0 <= t + 1 < n_blk:
                for j in inv_share[slot]:
                    inv_iter(t + 1, j)
                if slot == per_blk - 1:
                    apply_inverse(t + 1)
            if 0 <= t + 2 < n_blk:
                prep_head(t + 2, slot)
            if 0 <= t - 1 < n_blk:
                out_head(t - 1, slot)
    for hd in range(N_HEADS):
        s_ref[hd] = state[hd]


def _gdn(qkva, slab, gates_t, a_log, dt_bias, za, g_out, tt):
    B, T, _ = qkva.shape
    tok = lambda width: pl.BlockSpec((None, tt, width), lambda b, i: (b, i, 0))
    const = lambda shape: pl.BlockSpec(shape, lambda b, i: (0,) * len(shape))
    lane_vec = lambda v: jnp.zeros((1, LANES), F32).at[0, _L_A:_L_A + N_HEADS].set(v)
    sub_vec = lambda v: jnp.zeros((SUBLANES, 1), F32).at[0:N_HEADS, 0].set(v)
    masks = _gdn_masks()
    return pl.pallas_call(
        functools.partial(_gdn_kernel, tt=tt),
        grid=(B, T // tt),
        in_specs=[tok(3 * W_MIX), tok(LANES),
                  pl.BlockSpec((None, SUBLANES, tt), lambda b, i: (b, 0, i)),
                  const((1, LANES)), const((1, LANES)), const((SUBLANES, 1)), const((SUBLANES, 1))]
                 + [const(m.shape) for m in masks]
                 + [tok(W_MIX), const((1, HEAD_DIM))],
        out_specs=tok(W_MIX),
        out_shape=jax.ShapeDtypeStruct((B, T, W_MIX), BF16),
        scratch_shapes=[pltpu.VMEM((N_HEADS, HEAD_DIM, HEAD_DIM), F32),
                        pltpu.VMEM((N_HEADS, GDN_BLOCK, HEAD_DIM), F32)],
        compiler_params=pltpu.CompilerParams(dimension_semantics=("parallel", "arbitrary"),
                                             vmem_limit_bytes=VMEM_LIMIT),
        name="gdn",
    )(qkva, slab, gates_t, lane_vec(a_log), lane_vec(dt_bias), sub_vec(a_log), sub_vec(dt_bias),
      *masks, za, g_out.reshape(1, HEAD_DIM))


def _fox_kernel(fp_ref, qt_ref, k_ref, vt_ref, zf_ref, o_ref, m_ref, acc_ref, sa_ref, sb_ref, *, tq, tk, n_q):
    qi = pl.program_id(2)
    p0 = fp_ref[(pl.program_id(0) * N_HEADS + pl.program_id(1)) * n_q + qi]
    m_ref[...] = jnp.full_like(m_ref, NEG_INF)
    acc_ref[...] = jnp.zeros_like(acc_ref)
    ones_rows = jnp.ones((FOX_SUM_ROWS, tk), BF16)
    halves = (slice(0, tk), slice(tk, 2 * tk))

    def scores_into(buf, kb, which=(0, 1)):
        k = k_ref[pl.ds(pl.multiple_of(kb * tk, tk), tk), :]
        for h in which:
            buf[h] = jnp.dot(k, qt_ref[h], preferred_element_type=F32)

    def consume(buf, kb, modes):
        vt = jnp.concatenate([vt_ref[kb], ones_rows], axis=0)
        live = [h for h in (0, 1) if modes[h] != "skip"]
        pts, alphas = {}, {}
        for h in live:
            st = buf[h]
            if modes[h] == "diag":
                krow = lax.broadcasted_iota(jnp.int32, (tk, tk), 0)
                qcol = lax.broadcasted_iota(jnp.int32, (tk, tk), 1)
                st = jnp.where(qcol >= krow, st, NEG_INF)
            m_prev = m_ref[:, halves[h]]
            m_new = jnp.maximum(m_prev, jnp.max(st, axis=0, keepdims=True))
            alphas[h] = jnp.exp2(m_prev - m_new)
            pts[h] = jnp.exp2(st - m_new).astype(BF16)
            m_ref[:, halves[h]] = m_new
        for h in live:
            acc_ref[:, halves[h]] = (alphas[h] * acc_ref[:, halves[h]]
                                     + jnp.dot(vt, pts[h], preferred_element_type=F32))

    full = ("full", "full")
    scores_into(sa_ref, 2 * p0)

    def pair(p, carry):
        kb = 2 * p
        scores_into(sb_ref, kb + 1)
        consume(sa_ref, kb, full)
        scores_into(sa_ref, kb + 2)
        consume(sb_ref, kb + 1, full)
        return carry

    lax.fori_loop(p0, qi, pair, 0)
    scores_into(sb_ref, 2 * qi + 1, which=(1,))
    consume(sa_ref, 2 * qi, ("diag", "full"))
    consume(sb_ref, 2 * qi + 1, ("skip", "diag"))

    ot = acc_ref[0:HEAD_DIM, :] * (1.0 / acc_ref[HEAD_DIM:HEAD_DIM + 1, :])
    o_ref[...] = (ot.T * _silu(zf_ref[...].astype(F32))).astype(BF16)


def _fox_first_pair(stats, tk, n_q):
    c_first = stats[:, :, 0, _L_F:_L_F + N_HEADS]
    c_last = stats[:, :, 1, _L_F:_L_F + N_HEADS]
    qk_bound = jnp.sqrt(jnp.max(stats[:, :, 2, :N_HEADS], axis=1) * jnp.max(stats[:, :, 3, :N_HEADS], axis=1))
    n_kb = stats.shape[1]
    upper = (2.0 * qk_bound[:, None, None, :] + c_first[:, 0::2, None, :][:, :n_q]
             - c_last[:, None, :, :])
    kb = jnp.arange(n_kb, dtype=jnp.int32)[None, None, :, None]
    diag = 2 * jnp.arange(n_q, dtype=jnp.int32)[None, :, None, None]
    needed = jnp.logical_or(upper >= -FOX_SKIP_LOG2, kb >= diag)
    first_block = jnp.min(jnp.where(needed, kb, n_kb), axis=2)
    return jnp.right_shift(first_block, 1).transpose(0, 2, 1).reshape(-1)


def _fox(qf_aug, kf_aug, vft, zf, stats, tk):
    B, T, _ = zf.shape
    tq = 2 * tk
    assert T % tq == 0
    n_q = T // tq
    first_pair = _fox_first_pair(stats, tk, n_q)
    grid_spec = pltpu.PrefetchScalarGridSpec(
        num_scalar_prefetch=1,
        grid=(B, N_HEADS, n_q),
        in_specs=[pl.BlockSpec((None, tq // tk, 2 * HEAD_DIM, tk), lambda b, h, i, fp: (b, i, h, 0)),
                  pl.BlockSpec((None, T, 2 * HEAD_DIM), lambda b, h, i, fp: (b, 0, h)),
                  pl.BlockSpec((None, T // tk, HEAD_DIM, tk), lambda b, h, i, fp: (b, 0, h, 0)),
                  pl.BlockSpec((None, tq, HEAD_DIM), lambda b, h, i, fp: (b, i, h))],
        out_specs=pl.BlockSpec((None, tq, HEAD_DIM), lambda b, h, i, fp: (b, i, h)),
        scratch_shapes=[pltpu.VMEM((1, tq), F32), pltpu.VMEM((HEAD_DIM + FOX_SUM_ROWS, tq), F32),
                        pltpu.VMEM((2, tk, tk), F32), pltpu.VMEM((2, tk, tk), F32)])
    return pl.pallas_call(
        functools.partial(_fox_kernel, tq=tq, tk=tk, n_q=n_q),
        grid_spec=grid_spec,
        out_shape=jax.ShapeDtypeStruct((B, T, W_MIX), BF16),
        compiler_params=pltpu.CompilerParams(dimension_semantics=("parallel", "parallel", "arbitrary"),
                                             vmem_limit_bytes=VMEM_LIMIT),
        name="fox",
    )(first_pair, qf_aug, kf_aug, vft, zf)


def _outproj_kernel(oa_ref, of_ref, ga_ref, gf_ref, x_ref, gate_ref, wa_ref, wf_ref, wo_ref, o_ref):
    ya = jnp.dot(oa_ref[...], wa_ref[...], preferred_element_type=F32)
    yf = jnp.dot(of_ref[...], wf_ref[...], preferred_element_type=F32)
    merged = _sigmoid(ga_ref[...].astype(F32)) * ya + _sigmoid(gf_ref[...].astype(F32)) * yf
    res = jnp.dot(merged.astype(BF16), wo_ref[...], preferred_element_type=F32)
    o_ref[...] = x_ref[...] + gate_ref[...] * res


def _outproj(oa, of, ga, gf, x, gate, wa, wf, wo, tm):
    B, T, D = x.shape
    tok = lambda width: pl.BlockSpec((None, tm, width), lambda b, i: (b, i, 0))
    const = lambda shape: pl.BlockSpec(shape, lambda b, i: (0,) * len(shape))
    return pl.pallas_call(
        _outproj_kernel,
        grid=(B, T // tm),
        in_specs=[tok(W_MIX), tok(W_MIX), tok(D), tok(D), tok(D),
                  pl.BlockSpec((None, 1, D), lambda b, i: (b, 0, 0)),
                  const((W_MIX, D)), const((W_MIX, D)), const((D, D))],
        out_specs=tok(D),
        out_shape=jax.ShapeDtypeStruct((B, T, D), F32),
        compiler_params=pltpu.CompilerParams(dimension_semantics=("parallel", "parallel"),
                                             vmem_limit_bytes=VMEM_LIMIT),
        name="outproj",
    )(oa, of, ga, gf, x, gate, wa, wf, wo)


def _permute_w_in(w_in, d_model):
    o = 0
    cols = {}
    for name, width in (("qkva", 3 * W_MIX), ("a", N_HEADS), ("b", N_HEADS), ("za", W_MIX),
                        ("qf", W_MIX), ("kf", W_MIX), ("vf", W_MIX), ("f", N_HEADS), ("zf", W_MIX),
                        ("ga", d_model), ("gf", d_model)):
        cols[name] = w_in[:, o:o + width]
        o += width
    gates = jnp.concatenate([cols["a"], cols["b"], cols["f"],
                             jnp.zeros((w_in.shape[0], LANES - 3 * N_HEADS), w_in.dtype)], axis=1)
    return jnp.concatenate([cols["qkva"], cols["za"], cols["qf"], cols["kf"], cols["vf"], cols["zf"],
                            cols["ga"], cols["gf"], gates], axis=1).astype(BF16)


def kernel(x, c, w_ada, b_ada, g_norm, w_in, conv_w, A_log, dt_bias, g_gdn_out,
           g_q_fox, g_k_fox, b_f, w_o_gdn, w_o_fox, w_out):
    B, T, D = x.shape
    tm = tk = min(512, T)
    tt = min(4 * GDN_BLOCK, T)

    mod = _adaln(c, w_ada, b_ada)
    shift, scale, gate = (mod[:, j * D:(j + 1) * D].reshape(B, 1, D) for j in range(3))

    w_perm = _permute_w_in(w_in, D)
    bf_lane = jnp.zeros((1, LANES), F32).at[0, _L_F:_L_F + N_HEADS].set(b_f)
    (qkva, za, qf_aug, kf_aug, vft, zf, ga, gf, slab, stats) = _inproj(
        x, shift, scale, g_norm.reshape(1, D), w_perm, conv_w,
        g_q_fox.reshape(1, HEAD_DIM), g_k_fox.reshape(1, HEAD_DIM), bf_lane, tm)

    gates_t = jnp.transpose(slab[:, :, :SUBLANES], (0, 2, 1))
    oa = _gdn(qkva, slab, gates_t, A_log, dt_bias, za, g_gdn_out, tt)
    of = _fox(qf_aug, kf_aug, vft, zf, stats, tk)
    return _outproj(oa, of, ga, gf, x, gate, w_o_gdn.astype(BF16), w_o_fox.astype(BF16),
                    w_out.astype(BF16), min(2 * tm, T))
```

```python
import functools

import jax
import jax.numpy as jnp
from jax import lax
from jax.experimental import pallas as pl
from jax.experimental.pallas import tpu as pltpu

F32 = jnp.float32
BF16 = jnp.bfloat16
EPS = 1e-6
HEAD_DIM = 128
N_HEADS = 4
W_MIX = N_HEADS * HEAD_DIM
CONV_WIDTH = 4
GDN_CHUNK = 64
GDN_BLOCK = 256
MASK_BIG = 1e30
LANES = 128
MXU_TILE = 256
SUBLANES = 8
VMEM_LIMIT = 56 * 1024 * 1024
HI = lax.Precision.HIGHEST
NEG_INF = float("-inf")
LOG2E = 1.4426950408889634
FOX_SUM_ROWS = 16
FOX_SKIP_LOG2 = 152.0

_C_QKVA = 0
_C_ZA = 3 * W_MIX
_C_QF = _C_ZA + W_MIX
_C_KF = _C_QF + W_MIX
_C_VF = _C_KF + W_MIX
_C_ZF = _C_VF + W_MIX
_C_GA = _C_ZF + W_MIX
_L_A, _L_B, _L_F = 0, N_HEADS, 2 * N_HEADS


def _softplus(z):
    return jnp.maximum(z, 0.0) + jnp.log(1.0 + jnp.exp(-jnp.abs(z)))


def _log_sigmoid(z):
    return jnp.minimum(z, 0.0) - jnp.log(1.0 + jnp.exp(-jnp.abs(z)))


def _sigmoid(z):
    return 1.0 / (1.0 + jnp.exp(-z))


def _silu(z):
    return z * _sigmoid(z)


def _split3_bf16(c):
    hi = c.astype(BF16).astype(F32)
    r = c - hi
    mid = r.astype(BF16).astype(F32)
    lo = (r - mid).astype(BF16).astype(F32)
    return hi, mid, lo


def _adaln_kernel(c_ref, w_ref, b_ref, o_ref):
    o_ref[...] = jnp.dot(c_ref[...], w_ref[...], precision=HI, preferred_element_type=F32) + b_ref[...]


def _adaln(c, w_ada, b_ada):
    B, D = c.shape
    n = w_ada.shape[1]
    bn = D
    return pl.pallas_call(
        _adaln_kernel,
        grid=(n // bn,),
        in_specs=[pl.BlockSpec((B, D), lambda j: (0, 0)),
                  pl.BlockSpec((D, bn), lambda j: (0, j)),
                  pl.BlockSpec((1, bn), lambda j: (0, j))],
        out_specs=pl.BlockSpec((B, bn), lambda j: (0, j)),
        out_shape=jax.ShapeDtypeStruct((B, n), F32),
        compiler_params=pltpu.CompilerParams(dimension_semantics=("arbitrary",), vmem_limit_bytes=VMEM_LIMIT),
        name="adaln",
    )(c, w_ada, b_ada.reshape(1, n))


def _inproj_kernel(x_ref, shift_ref, scale_ref, gn_ref, w_ref, convw_ref, tri_ref,
                   gq_ref, gk_ref, bf_ref,
                   qkva_ref, za_ref, qf_ref, kf_ref, vft_ref, zf_ref, ga_ref, gf_ref, gates_ref, stats_ref,
                   carry_ref, xbuf_q, xbuf_k, xbuf_v, hbuf, gbuf, *, tm, d_model):
    i = pl.program_id(1)
    xbufs = (xbuf_q, xbuf_k, xbuf_v)

    @pl.when(i == 0)
    def _():
        carry_ref[...] = jnp.zeros_like(carry_ref)
        for xb in xbufs:
            xb[0:SUBLANES, :] = jnp.zeros((SUBLANES, W_MIX), F32)

    x = x_ref[...]
    ms = jnp.mean(x * x, axis=-1, keepdims=True)
    h = x * lax.rsqrt(ms + EPS) * gn_ref[...]
    h = h * (1.0 + scale_ref[...]) + shift_ref[...]
    hb = h.astype(BF16)

    def chunk_permuted(ref):
        return jnp.concatenate([ref[pl.ds(c * GDN_CHUNK + j, SUBLANES, stride=SUBLANES), :]
                                for c in range(tm // GDN_CHUNK) for j in range(SUBLANES)], axis=0)

    for kg in range(d_model // LANES):
        hbuf[kg] = h[:, kg * LANES:(kg + 1) * LANES]
    hbp = jnp.concatenate([chunk_permuted(hbuf.at[kg]) for kg in range(d_model // LANES)],
                          axis=1).astype(BF16)

    def proj(lo, width):
        return jnp.dot(hb, w_ref[:, lo:lo + width], preferred_element_type=F32)

    NT = MXU_TILE
    RC = tm // 4

    def qkva_tile(t):
        j, half = divmod(t, W_MIX // NT)
        xbufs[j][SUBLANES:SUBLANES + tm, half * NT:(half + 1) * NT] = jnp.dot(
            hbp, w_ref[:, _C_QKVA + t * NT:_C_QKVA + (t + 1) * NT], preferred_element_type=F32)

    def conv_silu(j, rc):
        xbuf = xbufs[j]
        cols = slice(j * W_MIX, (j + 1) * W_MIX)
        n_tail = CONV_WIDTH - 1
        for c in range(rc * RC // GDN_CHUNK, (rc + 1) * RC // GDN_CHUNK):
            b0 = SUBLANES + c * GDN_CHUNK
            wrapped = []
            for g in range(SUBLANES - n_tail, SUBLANES):
                prev = g if c == 0 else b0 - GDN_CHUNK + SUBLANES * g + SUBLANES - 1
                wrapped += [xbuf[prev:prev + 1, :], xbuf[b0 + SUBLANES * g:b0 + SUBLANES * (g + 1) - 1, :]]
            ext = jnp.concatenate(wrapped + [xbuf[b0:b0 + GDN_CHUNK, :]], axis=0)
            y = ext[0:GDN_CHUNK] * convw_ref[0:1, cols]
            for tap in range(1, CONV_WIDTH):
                y = y + ext[SUBLANES * tap:SUBLANES * tap + GDN_CHUNK] * convw_ref[tap:tap + 1, cols]
            qkva_ref[c * GDN_CHUNK:(c + 1) * GDN_CHUNK, cols] = _silu(y).astype(BF16)
        if rc == tm // RC - 1:
            last = SUBLANES + tm - GDN_CHUNK
            for g in range(SUBLANES - n_tail, SUBLANES):
                r = last + SUBLANES * g + SUBLANES - 1
                xbuf[g:g + 1, :] = xbuf[r:r + 1, :]

    def plain(out_ref, lo, t):
        out_ref[:, t * NT:(t + 1) * NT] = proj(lo + t * NT, NT).astype(BF16)

    def vft_tile(t):
        vft_ref[t * NT:(t + 1) * NT, :] = proj(_C_VF + t * NT, NT).astype(BF16).T

    qkva_tile(0)
    qkva_tile(1)
    for t in (2, 3, 4, 5):
        qkva_tile(t)
        conv_silu(0, t - 2)
    g = proj(_C_GA + 2 * d_model, LANES)
    conv_silu(1, 0)
    qk_tiles = {}
    for n, (name, lo) in enumerate((("q", _C_QF), ("q", _C_QF + NT), ("k", _C_KF), ("k", _C_KF + NT))):
        qk_tiles[(name, n % 2)] = proj(lo, NT)
        if n < 3:
            conv_silu(1, n + 1)

    gbuf[...] = g
    gates_ref[...] = chunk_permuted(gbuf)
    logf = _log_sigmoid(g + bf_ref[...])
    hi, mid, lo = _split3_bf16(logf)
    lane_g = lax.broadcasted_iota(jnp.int32, (tm, LANES), 1)
    packed = jnp.where(lane_g < _L_F + N_HEADS, hi,
                       jnp.where(lane_g < _L_F + 2 * N_HEADS, pltpu.roll(mid, N_HEADS, 1),
                                 jnp.where(lane_g < _L_F + 3 * N_HEADS, pltpu.roll(lo, 2 * N_HEADS, 1), 0.0)))
    cs = jnp.dot(tri_ref[...], packed.astype(BF16), preferred_element_type=F32)
    csum = (cs + pltpu.roll(cs, LANES - N_HEADS, 1)) + pltpu.roll(cs, LANES - 2 * N_HEADS, 1) + carry_ref[...]
    carry_ref[...] = csum[tm - 1:tm, :]

    for n, (ref, lo, t) in enumerate(((za_ref, _C_ZA, 0), (za_ref, _C_ZA, 1),
                                      (zf_ref, _C_ZF, 0), (zf_ref, _C_ZF, 1))):
        plain(ref, lo, t)
        conv_silu(2, n)

    HR = tm // 2
    lane = lax.broadcasted_iota(jnp.int32, (HR, LANES), 1)
    one = jnp.where(lane < 3, 0.0, jnp.where(lane < 6, 1.0, 0.0))
    one_k = jnp.where(lane < 3, 1.0, 0.0)

    def qk_epilogue(hd, half):
        rows = slice(half * HR, (half + 1) * HR)
        sl = slice((hd % 2) * HEAD_DIM, (hd % 2 + 1) * HEAD_DIM)
        qh = qk_tiles[("q", hd // 2)][rows, sl]
        kh = qk_tiles[("k", hd // 2)][rows, sl]
        qn = qh * lax.rsqrt(jnp.mean(qh * qh, axis=-1, keepdims=True) + EPS) * gq_ref[...]
        kn = kh * lax.rsqrt(jnp.mean(kh * kh, axis=-1, keepdims=True) + EPS) * gk_ref[...]
        ch = jnp.broadcast_to(csum[rows, _L_F + hd:_L_F + hd + 1], (HR, LANES)) * LOG2E
        hi, mid, lo = _split3_bf16(ch)
        aug_q = jnp.where(lane == 0, hi, jnp.where(lane == 1, mid, jnp.where(lane == 2, lo, one)))
        aug_k = jnp.where(lane == 3, -hi, jnp.where(lane == 4, -mid, jnp.where(lane == 5, -lo, one_k)))
        base = hd * 2 * HEAD_DIM
        qb = (qn * (HEAD_DIM ** -0.5 * LOG2E)).astype(BF16)
        kb = kn.astype(BF16)
        qf_ref[base:base + HEAD_DIM, rows] = qb.T
        qf_ref[base + HEAD_DIM:base + 2 * HEAD_DIM, rows] = aug_q.astype(BF16).T
        kf_ref[rows, base:base + HEAD_DIM] = kb
        kf_ref[rows, base + HEAD_DIM:base + 2 * HEAD_DIM] = aug_k.astype(BF16)
        for name, val in (("q", qb), ("k", kb)):
            v32 = val.astype(F32)
            sq_max[(name, hd, half)] = jnp.max(jnp.sum(v32 * v32, axis=-1, keepdims=True),
                                               axis=0, keepdims=True)

    sq_max = {}
    m_steps = ([functools.partial(vft_tile, t) for t in range(W_MIX // NT)]
               + [functools.partial(plain, ga_ref, _C_GA, t) for t in range(d_model // NT)]
               + [functools.partial(plain, gf_ref, _C_GA + d_model, t) for t in range(d_model // NT)])
    v_steps = [functools.partial(qk_epilogue, hd, half) for hd in range(N_HEADS) for half in range(2)]
    for n, m_step in enumerate(m_steps):
        m_step()
        if n < len(v_steps):
            v_steps[n]()

    lane1 = lax.broadcasted_iota(jnp.int32, (1, LANES), 1)
    norm_rows = []
    for name in ("q", "k"):
        row = jnp.zeros((1, LANES), F32)
        for hd in range(N_HEADS):
            row = jnp.where(lane1 == hd, jnp.maximum(sq_max[(name, hd, 0)], sq_max[(name, hd, 1)]), row)
        norm_rows.append(row)
    stats_ref[...] = jnp.concatenate(
        [csum[0:1, :] * LOG2E, csum[tm - 1:tm, :] * LOG2E] + norm_rows
        + [jnp.zeros((SUBLANES - 4, LANES), F32)], axis=0)


def _inproj(x, shift, scale, g_norm, w_perm, conv_w, g_q, g_k, bf_lane, tm):
    B, T, D = x.shape
    n_w = w_perm.shape[1]
    tri = jnp.tril(jnp.ones((tm, tm), BF16))
    tok = lambda width: pl.BlockSpec((None, tm, width), lambda b, i: (b, i, 0))
    per_b = pl.BlockSpec((None, 1, D), lambda b, i: (b, 0, 0))
    const = lambda shape: pl.BlockSpec(shape, lambda b, i: (0,) * len(shape))
    bf16_out = lambda w: (tok(w), jax.ShapeDtypeStruct((B, T, w), BF16))
    feat_major = lambda w: (pl.BlockSpec((None, None, w, tm), lambda b, i: (b, i, 0, 0)),
                            jax.ShapeDtypeStruct((B, T // tm, w, tm), BF16))
    outs = [bf16_out(3 * W_MIX), bf16_out(W_MIX), feat_major(2 * W_MIX), bf16_out(2 * W_MIX),
            feat_major(W_MIX),
            bf16_out(W_MIX), bf16_out(D), bf16_out(D),
            (tok(LANES), jax.ShapeDtypeStruct((B, T, LANES), F32)),
            (pl.BlockSpec((None, None, SUBLANES, LANES), lambda b, i: (b, i, 0, 0)),
             jax.ShapeDtypeStruct((B, T // tm, SUBLANES, LANES), F32))]
    return pl.pallas_call(
        functools.partial(_inproj_kernel, tm=tm, d_model=D),
        grid=(B, T // tm),
        in_specs=[tok(D), per_b, per_b, const((1, D)), const((D, n_w)),
                  const((CONV_WIDTH, 3 * W_MIX)), const((tm, tm)),
                  const((1, HEAD_DIM)), const((1, HEAD_DIM)), const((1, LANES))],
        out_specs=[o[0] for o in outs],
        out_shape=[o[1] for o in outs],
        scratch_shapes=([pltpu.VMEM((1, LANES), F32)] + [pltpu.VMEM((tm + SUBLANES, W_MIX), F32)] * 3
                        + [pltpu.VMEM((D // LANES, tm, LANES), F32), pltpu.VMEM((tm, LANES), F32)]),
        compiler_params=pltpu.CompilerParams(dimension_semantics=("parallel", "arbitrary"),
                                             vmem_limit_bytes=VMEM_LIMIT),
        name="inproj",
    )(x, shift, scale, g_norm, w_perm, conv_w, tri, g_q, g_k, bf_lane)


def _gdn_masks():
    def token(n):
        within = n % GDN_CHUNK
        return n - within + SUBLANES * (within % SUBLANES) + within // SUBLANES

    row = lax.broadcasted_iota(jnp.int32, (GDN_BLOCK, GDN_BLOCK), 0)
    col_ = lax.broadcasted_iota(jnp.int32, (GDN_BLOCK, GDN_BLOCK), 1)
    r, c = token(row), token(col_)
    same = (r // GDN_CHUNK) == (c // GDN_CHUNK)
    lower = jnp.logical_and(same, r >= c)
    f_lower = jnp.where(lower, 1.0, 0.0)
    f_same = jnp.where(same, 1.0, 0.0)
    f_cum = jnp.concatenate([f_lower, f_same], axis=0).astype(BF16)
    f_upper = jnp.where(jnp.logical_and(same, r <= c), 1.0, 0.0).astype(BF16)
    eye = jnp.where(r == c, 1.0, 0.0).astype(F32)
    nm_lower = jnp.where(lower, 0.0, -MASK_BIG).astype(F32)
    nm_strict = jnp.where(jnp.logical_and(same, r > c), 0.0, -MASK_BIG).astype(F32)
    return f_cum, f_upper, eye, nm_lower, nm_strict


def _gdn_kernel(qkv_ref, slab_ref, gt_ref, acol_ref, dcol_ref, arow_ref, drow_ref,
                fcum_ref, fupper_ref, eye_ref, nml_ref, nms_ref,
                za_ref, gout_ref, o_ref, s_ref, obuf, *, tt):
    i = pl.program_id(1)
    C = GDN_CHUNK
    BLK = GDN_BLOCK
    n_chunks = tt // C
    shift = C.bit_length() - 1

    @pl.when(i == 0)
    def _():
        s_ref[...] = jnp.zeros_like(s_ref)

    f_cum = fcum_ref[...]
    f_upper = fupper_ref[...]
    eye = eye_ref[...]
    nm_lower = nml_ref[...]
    nm_strict = nms_ref[...]
    n_blk = tt // BLK
    per_blk = BLK // C
    n_fac = shift - 1
    assert per_blk == N_HEADS

    def split2(v):
        hi = v.astype(BF16)
        return hi, (v - hi.astype(F32)).astype(BF16)

    def col(slab_, lane):
        return jnp.broadcast_to(slab_[:, lane:lane + 1], (slab_.shape[0], HEAD_DIM))

    gates = {}

    def prep_gates(sb):
        rows = slice(sb * BLK, (sb + 1) * BLK)
        slab = slab_ref[rows, :]
        g_col = -jnp.exp(acol_ref[...]) * _softplus(slab + dcol_ref[...])
        g_row = -jnp.exp(arow_ref[...]) * _softplus(gt_ref[:, rows] + drow_ref[...])
        g_hi, g_lo = split2(g_col)
        cum = jnp.dot(f_cum, jnp.concatenate([g_hi, g_lo], axis=1), preferred_element_type=F32)
        cum = cum[:, :LANES] + cum[:, LANES:]
        gc_col, gl_col = cum[:BLK], cum[BLK:]
        r_hi, r_lo = split2(g_row)
        cr = jnp.dot(jnp.concatenate([r_hi, r_lo], axis=0), f_upper, preferred_element_type=F32)
        gates[sb] = dict(beta=_sigmoid(slab), gc_col=gc_col, gc_row=cr[:SUBLANES] + cr[SUBLANES:],
                         e_gc=jnp.exp(gc_col), e_rem=jnp.exp(gl_col - gc_col), e_tot=jnp.exp(gl_col))

    qk_m, rhs_uw, qg_m, kd_m, etot_m, p_m, x_m, u_m, w_m = {}, {}, {}, {}, {}, {}, {}, {}, {}

    def prep_head(sb, hd):
        pr = (sb, hd)
        gt = gates[sb]
        rows = slice(sb * BLK, (sb + 1) * BLK)
        qh = qkv_ref[rows, hd * HEAD_DIM:(hd + 1) * HEAD_DIM].astype(F32)
        kh = qkv_ref[rows, W_MIX + hd * HEAD_DIM:W_MIX + (hd + 1) * HEAD_DIM].astype(F32)
        vh = qkv_ref[rows, 2 * W_MIX + hd * HEAD_DIM:2 * W_MIX + (hd + 1) * HEAD_DIM].astype(F32)
        qn = qh * lax.rsqrt(jnp.sum(qh * qh, axis=-1, keepdims=True) + EPS) * (HEAD_DIM ** -0.5)
        kn = kh * lax.rsqrt(jnp.sum(kh * kh, axis=-1, keepdims=True) + EPS)
        bh = col(gt["beta"], _L_B + hd)
        e_gc = col(gt["e_gc"], _L_A + hd)
        kb = kn * bh
        rhs_uw[pr] = jnp.concatenate([vh * bh, kb * e_gc], axis=1).astype(BF16)
        qg_m[pr] = (qn * e_gc).astype(BF16)
        kd_m[pr] = (kn * col(gt["e_rem"], _L_A + hd)).astype(BF16)
        etot_m[pr] = col(gt["e_tot"], _L_A + hd)
        kq = jnp.dot(jnp.concatenate([kb, qn], axis=0).astype(BF16), kn.T.astype(BF16),
                     preferred_element_type=F32)
        diff = gt["gc_col"][:, _L_A + hd:_L_A + hd + 1] - gt["gc_row"][hd:hd + 1, :]
        a = kq[:BLK] * jnp.exp(diff + nm_strict)
        qk_m[pr] = (kq[BLK:] * jnp.exp(diff + nm_lower)).astype(BF16)
        ab = a.astype(BF16)
        p_m[pr] = eye - a
        x_m[pr] = jnp.dot(ab, ab, preferred_element_type=F32)

    def inv_iter(sb, j):
        for hd in range(N_HEADS):
            pr = (sb, hd)
            xb = x_m[pr].astype(BF16)
            if j + 1 < n_fac:
                px = jnp.dot(jnp.concatenate([p_m[pr].astype(BF16), xb], axis=0), xb,
                             preferred_element_type=F32)
                p_m[pr] = p_m[pr] + px[:BLK]
                x_m[pr] = px[BLK:]
            else:
                p_m[pr] = p_m[pr] + jnp.dot(p_m[pr].astype(BF16), xb, preferred_element_type=F32)

    def apply_inverse(sb):
        for hd in range(N_HEADS):
            pr = (sb, hd)
            uw = jnp.dot(p_m[pr].astype(BF16), rhs_uw[pr], preferred_element_type=F32)
            u_m[pr] = uw[:, :HEAD_DIM]
            w_m[pr] = uw[:, HEAD_DIM:].astype(BF16)

    state = [s_ref[hd] for hd in range(N_HEADS)]
    vnew_parts, qs_parts = {}, {}

    def state_chunk(sb, cb):
        rows = slice(cb * C, (cb + 1) * C)
        for hd in range(N_HEADS):
            pr = (sb, hd)
            rr = jnp.dot(jnp.concatenate([w_m[pr][rows], qg_m[pr][rows]], axis=0),
                         state[hd].astype(BF16), preferred_element_type=F32)
            vnew = u_m[pr][rows] - rr[:C]
            qs_parts.setdefault(pr, []).append(rr[C:])
            vnew_parts.setdefault(pr, []).append(vnew)
            upd = lax.dot_general(kd_m[pr][rows], vnew.astype(BF16), (((0,), (0,)), ((), ())),
                                  preferred_element_type=F32)
            state[hd] = state[hd] * etot_m[pr][cb * C:cb * C + 1, :] + upd

    def out_head(sb, hd):
        pr = (sb, hd)
        rows = slice(sb * BLK, (sb + 1) * BLK)
        sl = slice(hd * HEAD_DIM, (hd + 1) * HEAD_DIM)
        vn = jnp.concatenate(vnew_parts[pr], axis=0).astype(BF16)
        o = jnp.concatenate(qs_parts[pr], axis=0) + jnp.dot(qk_m[pr], vn, preferred_element_type=F32)
        for g in range(BLK // SUBLANES):
            ch, j = divmod(g, C // SUBLANES)
            obuf[hd, pl.ds(ch * C + j, SUBLANES, stride=SUBLANES), :] = o[g * SUBLANES:(g + 1) * SUBLANES, :]
        o = obuf[hd]
        on = o * lax.rsqrt(jnp.mean(o * o, axis=-1, keepdims=True) + EPS) * gout_ref[...]
        o_ref[rows, sl] = (on * _silu(za_ref[rows, sl].astype(F32))).astype(BF16)

    inv_share = [[] for _ in range(per_blk)]
    for j in range(n_fac):
        inv_share[max(0, j - (n_fac - per_blk))].append(j)
    for t in range(-2, n_blk + 1):
        if 0 <= t + 2 < n_blk:
            prep_gates(t + 2)
        for slot in range(per_blk):
            if 0 <= t < n_blk:
                state_chunk(t, slot)
            if 0 <= t + 1 < n_blk:
                for j in inv_share[slot]:
                    inv_iter(t + 1, j)
                if slot == per_blk - 1:
                    apply_inverse(t + 1)
            if 0 <= t + 2 < n_blk:
                prep_head(t + 2, slot)
            if 0 <= t - 1 < n_blk:
                out_head(t - 1, slot)
    for hd in range(N_HEADS):
        s_ref[hd] = state[hd]


def _gdn(qkva, slab, gates_t, a_log, dt_bias, za, g_out, tt):
    B, T, _ = qkva.shape
    tok = lambda width: pl.BlockSpec((None, tt, width), lambda b, i: (b, i, 0))
    const = lambda shape: pl.BlockSpec(shape, lambda b, i: (0,) * len(shape))
    lane_vec = lambda v: jnp.zeros((1, LANES), F32).at[0, _L_A:_L_A + N_HEADS].set(v)
    sub_vec = lambda v: jnp.zeros((SUBLANES, 1), F32).at[0:N_HEADS, 0].set(v)
    masks = _gdn_masks()
    return pl.pallas_call(
        functools.partial(_gdn_kernel, tt=tt),
        grid=(B, T // tt),
        in_specs=[tok(3 * W_MIX), tok(LANES),
                  pl.BlockSpec((None, SUBLANES, tt), lambda b, i: (b, 0, i)),
                  const((1, LANES)), const((1, LANES)), const((SUBLANES, 1)), const((SUBLANES, 1))]
                 + [const(m.shape) for m in masks]
                 + [tok(W_MIX), const((1, HEAD_DIM))],
        out_specs=tok(W_MIX),
        out_shape=jax.ShapeDtypeStruct((B, T, W_MIX), BF16),
        scratch_shapes=[pltpu.VMEM((N_HEADS, HEAD_DIM, HEAD_DIM), F32),
                        pltpu.VMEM((N_HEADS, GDN_BLOCK, HEAD_DIM), F32)],
        compiler_params=pltpu.CompilerParams(dimension_semantics=("parallel", "arbitrary"),
                                             vmem_limit_bytes=VMEM_LIMIT),
        name="gdn",
    )(qkva, slab, gates_t, lane_vec(a_log), lane_vec(dt_bias), sub_vec(a_log), sub_vec(dt_bias),
      *masks, za, g_out.reshape(1, HEAD_DIM))


def _fox_kernel(fp_ref, qt_ref, k_ref, vt_ref, zf_ref, o_ref, m_ref, acc_ref, sa_ref, sb_ref, *, tq, tk, n_q):
    first_pair_base = (pl.program_id(0) * N_HEADS + pl.program_id(1)) * n_q
    ones_rows = jnp.ones((FOX_SUM_ROWS, tk), BF16)
    halves = (slice(0, tk), slice(tk, 2 * tk))

    def scores_into(buf, kb, qi, which=(0, 1)):
        k = k_ref[pl.ds(pl.multiple_of(kb * tk, tk), tk), :]
        for h in which:
            buf[h] = jnp.dot(k, qt_ref[2 * qi + h], preferred_element_type=F32)

    def consume(buf, kb, modes):
        vt = jnp.concatenate([vt_ref[kb], ones_rows], axis=0)
        live = [h for h in (0, 1) if modes[h] != "skip"]
        pts, alphas = {}, {}
        for h in live:
            st = buf[h]
            if modes[h] == "diag":
                krow = lax.broadcasted_iota(jnp.int32, (tk, tk), 0)
                qcol = lax.broadcasted_iota(jnp.int32, (tk, tk), 1)
                st = jnp.where(qcol >= krow, st, NEG_INF)
            m_prev = m_ref[:, halves[h]]
            m_new = jnp.maximum(m_prev, jnp.max(st, axis=0, keepdims=True))
            alphas[h] = jnp.exp2(m_prev - m_new)
            pts[h] = jnp.exp2(st - m_new).astype(BF16)
            m_ref[:, halves[h]] = m_new
        for h in live:
            acc_ref[:, halves[h]] = (alphas[h] * acc_ref[:, halves[h]]
                                     + jnp.dot(vt, pts[h], preferred_element_type=F32))

    full = ("full", "full")
    p0 = fp_ref[first_pair_base]
    scores_into(sa_ref, 2 * p0, 0)
    for qi in range(n_q):
        m_ref[...] = jnp.full_like(m_ref, NEG_INF)
        acc_ref[...] = jnp.zeros_like(acc_ref)

        def pair(p, carry, qi=qi):
            kb = 2 * p
            scores_into(sb_ref, kb + 1, qi)
            consume(sa_ref, kb, full)
            scores_into(sa_ref, kb + 2, qi)
            consume(sb_ref, kb + 1, full)
            return carry

        lax.fori_loop(p0, qi, pair, 0)
        scores_into(sb_ref, 2 * qi + 1, qi, which=(1,))
        consume(sa_ref, 2 * qi, ("diag", "full"))
        if qi + 1 < n_q:
            p0 = fp_ref[first_pair_base + qi + 1]
            scores_into(sa_ref, 2 * p0, qi + 1)
        consume(sb_ref, 2 * qi + 1, ("skip", "diag"))

        rows = slice(qi * tq, (qi + 1) * tq)
        ot = acc_ref[0:HEAD_DIM, :] * (1.0 / acc_ref[HEAD_DIM:HEAD_DIM + 1, :])
        o_ref[rows, :] = (ot.T * _silu(zf_ref[rows, :].astype(F32))).astype(BF16)


def _fox_first_pair(stats, tk, n_q):
    c_first = stats[:, :, 0, _L_F:_L_F + N_HEADS]
    c_last = stats[:, :, 1, _L_F:_L_F + N_HEADS]
    qk_bound = jnp.sqrt(jnp.max(stats[:, :, 2, :N_HEADS], axis=1) * jnp.max(stats[:, :, 3, :N_HEADS], axis=1))
    n_kb = stats.shape[1]
    upper = (2.0 * qk_bound[:, None, None, :] + c_first[:, 0::2, None, :][:, :n_q]
             - c_last[:, None, :, :])
    kb = jnp.arange(n_kb, dtype=jnp.int32)[None, None, :, None]
    diag = 2 * jnp.arange(n_q, dtype=jnp.int32)[None, :, None, None]
    needed = jnp.logical_or(upper >= -FOX_SKIP_LOG2, kb >= diag)
    first_block = jnp.min(jnp.where(needed, kb, n_kb), axis=2)
    return jnp.right_shift(first_block, 1).transpose(0, 2, 1).reshape(-1)


def _fox(qf_aug, kf_aug, vft, zf, stats, tk):
    B, T, _ = zf.shape
    tq = 2 * tk
    assert T % tq == 0
    n_q = T // tq
    first_pair = _fox_first_pair(stats, tk, n_q)
    grid_spec = pltpu.PrefetchScalarGridSpec(
        num_scalar_prefetch=1,
        grid=(B, N_HEADS),
        in_specs=[pl.BlockSpec((None, T // tk, 2 * HEAD_DIM, tk), lambda b, h, fp: (b, 0, h, 0)),
                  pl.BlockSpec((None, T, 2 * HEAD_DIM), lambda b, h, fp: (b, 0, h)),
                  pl.BlockSpec((None, T // tk, HEAD_DIM, tk), lambda b, h, fp: (b, 0, h, 0)),
                  pl.BlockSpec((None, T, HEAD_DIM), lambda b, h, fp: (b, 0, h))],
        out_specs=pl.BlockSpec((None, T, HEAD_DIM), lambda b, h, fp: (b, 0, h)),
        scratch_shapes=[pltpu.VMEM((1, tq), F32), pltpu.VMEM((HEAD_DIM + FOX_SUM_ROWS, tq), F32),
                        pltpu.VMEM((2, tk, tk), F32), pltpu.VMEM((2, tk, tk), F32)])
    return pl.pallas_call(
        functools.partial(_fox_kernel, tq=tq, tk=tk, n_q=n_q),
        grid_spec=grid_spec,
        out_shape=jax.ShapeDtypeStruct((B, T, W_MIX), BF16),
        compiler_params=pltpu.CompilerParams(dimension_semantics=("parallel", "parallel"),
                                             vmem_limit_bytes=VMEM_LIMIT),
        name="fox",
    )(first_pair, qf_aug, kf_aug, vft, zf)


def _outproj_kernel(oa_ref, of_ref, ga_ref, gf_ref, x_ref, gate_ref, wa_ref, wf_ref, wo_ref, o_ref):
    ya = jnp.dot(oa_ref[...], wa_ref[...], preferred_element_type=F32)
    yf = jnp.dot(of_ref[...], wf_ref[...], preferred_element_type=F32)
    merged = _sigmoid(ga_ref[...].astype(F32)) * ya + _sigmoid(gf_ref[...].astype(F32)) * yf
    res = jnp.dot(merged.astype(BF16), wo_ref[...], preferred_element_type=F32)
    o_ref[...] = x_ref[...] + gate_ref[...] * res


def _outproj(oa, of, ga, gf, x, gate, wa, wf, wo, tm):
    B, T, D = x.shape
    tok = lambda width: pl.BlockSpec((None, tm, width), lambda b, i: (b, i, 0))
    const = lambda shape: pl.BlockSpec(shape, lambda b, i: (0,) * len(shape))
    return pl.pallas_call(
        _outproj_kernel,
        grid=(B, T // tm),
        in_specs=[tok(W_MIX), tok(W_MIX), tok(D), tok(D), tok(D),
                  pl.BlockSpec((None, 1, D), lambda b, i: (b, 0, 0)),
                  const((W_MIX, D)), const((W_MIX, D)), const((D, D))],
        out_specs=tok(D),
        out_shape=jax.ShapeDtypeStruct((B, T, D), F32),
        compiler_params=pltpu.CompilerParams(dimension_semantics=("parallel", "parallel"),
                                             vmem_limit_bytes=VMEM_LIMIT),
        name="outproj",
    )(oa, of, ga, gf, x, gate, wa, wf, wo)


def _permute_w_in(w_in, d_model):
    o = 0
    cols = {}
    for name, width in (("qkva", 3 * W_MIX), ("a", N_HEADS), ("b", N_HEADS), ("za", W_MIX),
                        ("qf", W_MIX), ("kf", W_MIX), ("vf", W_MIX), ("f", N_HEADS), ("zf", W_MIX),
                        ("ga", d_model), ("gf", d_model)):
        cols[name] = w_in[:, o:o + width]
        o += width
    gates = jnp.concatenate([cols["a"], cols["b"], cols["f"],
                             jnp.zeros((w_in.shape[0], LANES - 3 * N_HEADS), w_in.dtype)], axis=1)
    return jnp.concatenate([cols["qkva"], cols["za"], cols["qf"], cols["kf"], cols["vf"], cols["zf"],
                            cols["ga"], cols["gf"], gates], axis=1).astype(BF16)


def kernel(x, c, w_ada, b_ada, g_norm, w_in, conv_w, A_log, dt_bias, g_gdn_out,
           g_q_fox, g_k_fox, b_f, w_o_gdn, w_o_fox, w_out):
    B, T, D = x.shape
    tm = tk = min(512, T)
    tt = min(4 * GDN_BLOCK, T)

    mod = _adaln(c, w_ada, b_ada)
    shift, scale, gate = (mod[:, j * D:(j + 1) * D].reshape(B, 1, D) for j in range(3))

    w_perm = _permute_w_in(w_in, D)
    bf_lane = jnp.zeros((1, LANES), F32).at[0, _L_F:_L_F + N_HEADS].set(b_f)
    (qkva, za, qf_aug, kf_aug, vft, zf, ga, gf, slab, stats) = _inproj(
        x, shift, scale, g_norm.reshape(1, D), w_perm, conv_w,
        g_q_fox.reshape(1, HEAD_DIM), g_k_fox.reshape(1, HEAD_DIM), bf_lane, tm)

    gates_t = jnp.transpose(slab[:, :, :SUBLANES], (0, 2, 1))
    oa = _gdn(qkva, slab, gates_t, A_log, dt_bias, za, g_gdn_out, tt)
    of = _fox(qf_aug, kf_aug, vft, zf, stats, tk)
    return _outproj(oa, of, ga, gf, x, gate, w_o_gdn.astype(BF16), w_o_fox.astype(BF16),
                    w_out.astype(BF16), min(2 * tm, T))
```

```python
import functools

import jax
import jax.numpy as jnp
from jax import lax
from jax.experimental import pallas as pl
from jax.experimental.pallas import tpu as pltpu

F32 = jnp.float32
BF16 = jnp.bfloat16
EPS = 1e-6
HEAD_DIM = 128
N_HEADS = 4
W_MIX = N_HEADS * HEAD_DIM
CONV_WIDTH = 4
GDN_CHUNK = 64
GDN_BLOCK = 256
MASK_BIG = 1e30
LANES = 128
MXU_TILE = 256
SUBLANES = 8
VMEM_LIMIT = 56 * 1024 * 1024
HI = lax.Precision.HIGHEST
NEG_INF = float("-inf")
LOG2E = 1.4426950408889634
FOX_SUM_ROWS = 16
FOX_SKIP_LOG2 = 152.0

_C_QKVA = 0
_C_ZA = 3 * W_MIX
_C_QF = _C_ZA + W_MIX
_C_KF = _C_QF + W_MIX
_C_VF = _C_KF + W_MIX
_C_ZF = _C_VF + W_MIX
_C_GA = _C_ZF + W_MIX
_L_A, _L_B, _L_F = 0, N_HEADS, 2 * N_HEADS


def _softplus(z):
    return jnp.maximum(z, 0.0) + jnp.log(1.0 + jnp.exp(-jnp.abs(z)))


def _log_sigmoid(z):
    return jnp.minimum(z, 0.0) - jnp.log(1.0 + jnp.exp(-jnp.abs(z)))


def _sigmoid(z):
    return 1.0 / (1.0 + jnp.exp(-z))


def _silu(z):
    return z * _sigmoid(z)


def _split3_bf16(c):
    hi = c.astype(BF16).astype(F32)
    r = c - hi
    mid = r.astype(BF16).astype(F32)
    lo = (r - mid).astype(BF16).astype(F32)
    return hi, mid, lo


def _adaln_kernel(c_ref, w_ref, b_ref, o_ref):
    o_ref[...] = jnp.dot(c_ref[...], w_ref[...], precision=HI, preferred_element_type=F32) + b_ref[...]


def _adaln(c, w_ada, b_ada):
    B, D = c.shape
    n = w_ada.shape[1]
    bn = D
    return pl.pallas_call(
        _adaln_kernel,
        grid=(n // bn,),
        in_specs=[pl.BlockSpec((B, D), lambda j: (0, 0)),
                  pl.BlockSpec((D, bn), lambda j: (0, j)),
                  pl.BlockSpec((1, bn), lambda j: (0, j))],
        out_specs=pl.BlockSpec((B, bn), lambda j: (0, j)),
        out_shape=jax.ShapeDtypeStruct((B, n), F32),
        compiler_params=pltpu.CompilerParams(dimension_semantics=("arbitrary",), vmem_limit_bytes=VMEM_LIMIT),
        name="adaln",
    )(c, w_ada, b_ada.reshape(1, n))


def _inproj_kernel(x_ref, shift_ref, scale_ref, gn_ref, w_ref, convw_ref, tri_ref,
                   gq_ref, gk_ref, bf_ref,
                   qkva_ref, za_ref, qf_ref, kf_ref, vft_ref, zf_ref, ga_ref, gf_ref, gates_ref, stats_ref,
                   carry_ref, xbuf_q, xbuf_k, xbuf_v, hbuf, gbuf, *, tm, d_model):
    i = pl.program_id(1)
    xbufs = (xbuf_q, xbuf_k, xbuf_v)

    @pl.when(i == 0)
    def _():
        carry_ref[...] = jnp.zeros_like(carry_ref)
        for xb in xbufs:
            xb[0:SUBLANES, :] = jnp.zeros((SUBLANES, W_MIX), F32)

    x = x_ref[...]
    ms = jnp.mean(x * x, axis=-1, keepdims=True)
    h = x * lax.rsqrt(ms + EPS) * gn_ref[...]
    h = h * (1.0 + scale_ref[...]) + shift_ref[...]
    hb = h.astype(BF16)

    def chunk_permuted(ref):
        return jnp.concatenate([ref[pl.ds(c * GDN_CHUNK + j, SUBLANES, stride=SUBLANES), :]
                                for c in range(tm // GDN_CHUNK) for j in range(SUBLANES)], axis=0)

    for kg in range(d_model // LANES):
        hbuf[kg] = h[:, kg * LANES:(kg + 1) * LANES]
    hbp = jnp.concatenate([chunk_permuted(hbuf.at[kg]) for kg in range(d_model // LANES)],
                          axis=1).astype(BF16)

    def proj(lo, width):
        return jnp.dot(hb, w_ref[:, lo:lo + width], preferred_element_type=F32)

    NT = MXU_TILE
    RC = tm // 4

    def qkva_tile(t):
        j, half = divmod(t, W_MIX // NT)
        xbufs[j][SUBLANES:SUBLANES + tm, half * NT:(half + 1) * NT] = jnp.dot(
            hbp, w_ref[:, _C_QKVA + t * NT:_C_QKVA + (t + 1) * NT], preferred_element_type=F32)

    def conv_silu(j, rc):
        xbuf = xbufs[j]
        cols = slice(j * W_MIX, (j + 1) * W_MIX)
        n_tail = CONV_WIDTH - 1
        for c in range(rc * RC // GDN_CHUNK, (rc + 1) * RC // GDN_CHUNK):
            b0 = SUBLANES + c * GDN_CHUNK
            wrapped = []
            for g in range(SUBLANES - n_tail, SUBLANES):
                prev = g if c == 0 else b0 - GDN_CHUNK + SUBLANES * g + SUBLANES - 1
                wrapped += [xbuf[prev:prev + 1, :], xbuf[b0 + SUBLANES * g:b0 + SUBLANES * (g + 1) - 1, :]]
            ext = jnp.concatenate(wrapped + [xbuf[b0:b0 + GDN_CHUNK, :]], axis=0)
            y = ext[0:GDN_CHUNK] * convw_ref[0:1, cols]
            for tap in range(1, CONV_WIDTH):
                y = y + ext[SUBLANES * tap:SUBLANES * tap + GDN_CHUNK] * convw_ref[tap:tap + 1, cols]
            qkva_ref[c * GDN_CHUNK:(c + 1) * GDN_CHUNK, cols] = _silu(y).astype(BF16)
        if rc == tm // RC - 1:
            last = SUBLANES + tm - GDN_CHUNK
            for g in range(SUBLANES - n_tail, SUBLANES):
                r = last + SUBLANES * g + SUBLANES - 1
                xbuf[g:g + 1, :] = xbuf[r:r + 1, :]

    def plain(out_ref, lo, t):
        out_ref[:, t * NT:(t + 1) * NT] = proj(lo + t * NT, NT).astype(BF16)

    heads_per_tile = NT // HEAD_DIM

    def vft_tile(t):
        vt = proj(_C_VF + t * NT, NT).astype(BF16).T
        for n in range(heads_per_tile):
            vft_ref[heads_per_tile * t + n] = vt[n * HEAD_DIM:(n + 1) * HEAD_DIM, :]

    def zf_tile(t):
        z = proj(_C_ZF + t * NT, NT).astype(BF16)
        for n in range(heads_per_tile):
            zf_ref[heads_per_tile * t + n] = z[:, n * HEAD_DIM:(n + 1) * HEAD_DIM]

    qkva_tile(0)
    qkva_tile(1)
    for t in (2, 3, 4, 5):
        qkva_tile(t)
        conv_silu(0, t - 2)
    g = proj(_C_GA + 2 * d_model, LANES)
    conv_silu(1, 0)
    qk_tiles = {}
    for n, (name, lo) in enumerate((("q", _C_QF), ("q", _C_QF + NT), ("k", _C_KF), ("k", _C_KF + NT))):
        qk_tiles[(name, n % 2)] = proj(lo, NT)
        if n < 3:
            conv_silu(1, n + 1)

    gbuf[...] = g
    gates_ref[...] = chunk_permuted(gbuf)
    logf = _log_sigmoid(g + bf_ref[...])
    hi, mid, lo = _split3_bf16(logf)
    lane_g = lax.broadcasted_iota(jnp.int32, (tm, LANES), 1)
    packed = jnp.where(lane_g < _L_F + N_HEADS, hi,
                       jnp.where(lane_g < _L_F + 2 * N_HEADS, pltpu.roll(mid, N_HEADS, 1),
                                 jnp.where(lane_g < _L_F + 3 * N_HEADS, pltpu.roll(lo, 2 * N_HEADS, 1), 0.0)))
    cs = jnp.dot(tri_ref[...], packed.astype(BF16), preferred_element_type=F32)
    csum = (cs + pltpu.roll(cs, LANES - N_HEADS, 1)) + pltpu.roll(cs, LANES - 2 * N_HEADS, 1) + carry_ref[...]
    carry_ref[...] = csum[tm - 1:tm, :]

    for n, m_step in enumerate((functools.partial(plain, za_ref, _C_ZA, 0),
                                functools.partial(plain, za_ref, _C_ZA, 1),
                                functools.partial(zf_tile, 0), functools.partial(zf_tile, 1))):
        m_step()
        conv_silu(2, n)

    HR = tm // 2
    lane = lax.broadcasted_iota(jnp.int32, (HR, LANES), 1)
    one = jnp.where(lane < 3, 0.0, jnp.where(lane < 6, 1.0, 0.0))
    one_k = jnp.where(lane < 3, 1.0, 0.0)

    def qk_epilogue(hd, half):
        rows = slice(half * HR, (half + 1) * HR)
        sl = slice((hd % 2) * HEAD_DIM, (hd % 2 + 1) * HEAD_DIM)
        qh = qk_tiles[("q", hd // 2)][rows, sl]
        kh = qk_tiles[("k", hd // 2)][rows, sl]
        qn = qh * lax.rsqrt(jnp.mean(qh * qh, axis=-1, keepdims=True) + EPS) * gq_ref[...]
        kn = kh * lax.rsqrt(jnp.mean(kh * kh, axis=-1, keepdims=True) + EPS) * gk_ref[...]
        ch = jnp.broadcast_to(csum[rows, _L_F + hd:_L_F + hd + 1], (HR, LANES)) * LOG2E
        hi, mid, lo = _split3_bf16(ch)
        aug_q = jnp.where(lane == 0, hi, jnp.where(lane == 1, mid, jnp.where(lane == 2, lo, one)))
        aug_k = jnp.where(lane == 3, -hi, jnp.where(lane == 4, -mid, jnp.where(lane == 5, -lo, one_k)))
        base = hd * 2 * HEAD_DIM
        qb = (qn * (HEAD_DIM ** -0.5 * LOG2E)).astype(BF16)
        kb = kn.astype(BF16)
        qf_ref[hd, 0:HEAD_DIM, rows] = qb.T
        qf_ref[hd, HEAD_DIM:2 * HEAD_DIM, rows] = aug_q.astype(BF16).T
        kf_ref[hd, rows, 0:HEAD_DIM] = kb
        kf_ref[hd, rows, HEAD_DIM:2 * HEAD_DIM] = aug_k.astype(BF16)
        for name, val in (("q", qb), ("k", kb)):
            v32 = val.astype(F32)
            sq_max[(name, hd, half)] = jnp.max(jnp.sum(v32 * v32, axis=-1, keepdims=True),
                                               axis=0, keepdims=True)

    sq_max = {}
    m_steps = ([functools.partial(vft_tile, t) for t in range(W_MIX // NT)]
               + [functools.partial(plain, ga_ref, _C_GA, t) for t in range(d_model // NT)]
               + [functools.partial(plain, gf_ref, _C_GA + d_model, t) for t in range(d_model // NT)])
    v_steps = [functools.partial(qk_epilogue, hd, half) for hd in range(N_HEADS) for half in range(2)]
    for n, m_step in enumerate(m_steps):
        m_step()
        if n < len(v_steps):
            v_steps[n]()

    lane1 = lax.broadcasted_iota(jnp.int32, (1, LANES), 1)
    norm_rows = []
    for name in ("q", "k"):
        row = jnp.zeros((1, LANES), F32)
        for hd in range(N_HEADS):
            row = jnp.where(lane1 == hd, jnp.maximum(sq_max[(name, hd, 0)], sq_max[(name, hd, 1)]), row)
        norm_rows.append(row)
    stats_ref[...] = jnp.concatenate(
        [csum[0:1, :] * LOG2E, csum[tm - 1:tm, :] * LOG2E] + norm_rows
        + [jnp.zeros((SUBLANES - 4, LANES), F32)], axis=0)


def _inproj(x, shift, scale, g_norm, w_perm, conv_w, g_q, g_k, bf_lane, tm):
    B, T, D = x.shape
    n_w = w_perm.shape[1]
    tri = jnp.tril(jnp.ones((tm, tm), BF16))
    tok = lambda width: pl.BlockSpec((None, tm, width), lambda b, i: (b, i, 0))
    per_b = pl.BlockSpec((None, 1, D), lambda b, i: (b, 0, 0))
    const = lambda shape: pl.BlockSpec(shape, lambda b, i: (0,) * len(shape))
    bf16_out = lambda w: (tok(w), jax.ShapeDtypeStruct((B, T, w), BF16))
    head_tok = lambda w: (pl.BlockSpec((None, N_HEADS, tm, w), lambda b, i: (b, 0, i, 0)),
                          jax.ShapeDtypeStruct((B, N_HEADS, T, w), BF16))
    head_feat = lambda w: (pl.BlockSpec((None, N_HEADS, None, w, tm), lambda b, i: (b, 0, i, 0, 0)),
                           jax.ShapeDtypeStruct((B, N_HEADS, T // tm, w, tm), BF16))
    outs = [bf16_out(3 * W_MIX), bf16_out(W_MIX), head_feat(2 * HEAD_DIM), head_tok(2 * HEAD_DIM),
            head_feat(HEAD_DIM),
            head_tok(HEAD_DIM), bf16_out(D), bf16_out(D),
            (tok(LANES), jax.ShapeDtypeStruct((B, T, LANES), F32)),
            (pl.BlockSpec((None, None, SUBLANES, LANES), lambda b, i: (b, i, 0, 0)),
             jax.ShapeDtypeStruct((B, T // tm, SUBLANES, LANES), F32))]
    return pl.pallas_call(
        functools.partial(_inproj_kernel, tm=tm, d_model=D),
        grid=(B, T // tm),
        in_specs=[tok(D), per_b, per_b, const((1, D)), const((D, n_w)),
                  const((CONV_WIDTH, 3 * W_MIX)), const((tm, tm)),
                  const((1, HEAD_DIM)), const((1, HEAD_DIM)), const((1, LANES))],
        out_specs=[o[0] for o in outs],
        out_shape=[o[1] for o in outs],
        scratch_shapes=([pltpu.VMEM((1, LANES), F32)] + [pltpu.VMEM((tm + SUBLANES, W_MIX), F32)] * 3
                        + [pltpu.VMEM((D // LANES, tm, LANES), F32), pltpu.VMEM((tm, LANES), F32)]),
        compiler_params=pltpu.CompilerParams(dimension_semantics=("parallel", "arbitrary"),
                                             vmem_limit_bytes=VMEM_LIMIT),
        name="inproj",
    )(x, shift, scale, g_norm, w_perm, conv_w, tri, g_q, g_k, bf_lane)


def _gdn_masks():
    def token(n):
        within = n % GDN_CHUNK
        return n - within + SUBLANES * (within % SUBLANES) + within // SUBLANES

    row = lax.broadcasted_iota(jnp.int32, (GDN_BLOCK, GDN_BLOCK), 0)
    col_ = lax.broadcasted_iota(jnp.int32, (GDN_BLOCK, GDN_BLOCK), 1)
    r, c = token(row), token(col_)
    same = (r // GDN_CHUNK) == (c // GDN_CHUNK)
    lower = jnp.logical_and(same, r >= c)
    f_lower = jnp.where(lower, 1.0, 0.0)
    f_same = jnp.where(same, 1.0, 0.0)
    f_cum = jnp.concatenate([f_lower, f_same], axis=0).astype(BF16)
    f_upper = jnp.where(jnp.logical_and(same, r <= c), 1.0, 0.0).astype(BF16)
    eye = jnp.where(r == c, 1.0, 0.0).astype(F32)
    nm_lower = jnp.where(lower, 0.0, -MASK_BIG).astype(F32)
    nm_strict = jnp.where(jnp.logical_and(same, r > c), 0.0, -MASK_BIG).astype(F32)
    return f_cum, f_upper, eye, nm_lower, nm_strict


def _gdn_kernel(qkv_ref, slab_ref, gt_ref, acol_ref, dcol_ref, arow_ref, drow_ref,
                fcum_ref, fupper_ref, eye_ref, nml_ref, nms_ref,
                za_ref, gout_ref, o_ref, s_ref, obuf, *, tt):
    i = pl.program_id(1)
    C = GDN_CHUNK
    BLK = GDN_BLOCK
    n_chunks = tt // C
    shift = C.bit_length() - 1

    @pl.when(i == 0)
    def _():
        s_ref[...] = jnp.zeros_like(s_ref)

    f_cum = fcum_ref[...]
    f_upper = fupper_ref[...]
    eye = eye_ref[...]
    nm_lower = nml_ref[...]
    nm_strict = nms_ref[...]
    n_blk = tt // BLK
    per_blk = BLK // C
    n_fac = shift - 1
    assert per_blk == N_HEADS

    def split2(v):
        hi = v.astype(BF16)
        return hi, (v - hi.astype(F32)).astype(BF16)

    def col(slab_, lane):
        return jnp.broadcast_to(slab_[:, lane:lane + 1], (slab_.shape[0], HEAD_DIM))

    gates = {}

    def prep_gates(sb):
        rows = slice(sb * BLK, (sb + 1) * BLK)
        slab = slab_ref[rows, :]
        g_col = -jnp.exp(acol_ref[...]) * _softplus(slab + dcol_ref[...])
        g_row = -jnp.exp(arow_ref[...]) * _softplus(gt_ref[:, rows] + drow_ref[...])
        g_hi, g_lo = split2(g_col)
        cum = jnp.dot(f_cum, jnp.concatenate([g_hi, g_lo], axis=1), preferred_element_type=F32)
        cum = cum[:, :LANES] + cum[:, LANES:]
        gc_col, gl_col = cum[:BLK], cum[BLK:]
        r_hi, r_lo = split2(g_row)
        cr = jnp.dot(jnp.concatenate([r_hi, r_lo], axis=0), f_upper, preferred_element_type=F32)
        gates[sb] = dict(beta=_sigmoid(slab), gc_col=gc_col, gc_row=cr[:SUBLANES] + cr[SUBLANES:],
                         e_gc=jnp.exp(gc_col), e_rem=jnp.exp(gl_col - gc_col), e_tot=jnp.exp(gl_col))

    qk_m, rhs_uw, qg_m, kd_m, etot_m, p_m, x_m, u_m, w_m = {}, {}, {}, {}, {}, {}, {}, {}, {}

    def prep_head(sb, hd):
        pr = (sb, hd)
        gt = gates[sb]
        rows = slice(sb * BLK, (sb + 1) * BLK)
        qh = qkv_ref[rows, hd * HEAD_DIM:(hd + 1) * HEAD_DIM].astype(F32)
        kh = qkv_ref[rows, W_MIX + hd * HEAD_DIM:W_MIX + (hd + 1) * HEAD_DIM].astype(F32)
        vh = qkv_ref[rows, 2 * W_MIX + hd * HEAD_DIM:2 * W_MIX + (hd + 1) * HEAD_DIM].astype(F32)
        qn = qh * lax.rsqrt(jnp.sum(qh * qh, axis=-1, keepdims=True) + EPS) * (HEAD_DIM ** -0.5)
        kn = kh * lax.rsqrt(jnp.sum(kh * kh, axis=-1, keepdims=True) + EPS)
        bh = col(gt["beta"], _L_B + hd)
        e_gc = col(gt["e_gc"], _L_A + hd)
        kb = kn * bh
        rhs_uw[pr] = jnp.concatenate([vh * bh, kb * e_gc], axis=1).astype(BF16)
        qg_m[pr] = (qn * e_gc).astype(BF16)
        kd_m[pr] = (kn * col(gt["e_rem"], _L_A + hd)).astype(BF16)
        etot_m[pr] = col(gt["e_tot"], _L_A + hd)
        kq = jnp.dot(jnp.concatenate([kb, qn], axis=0).astype(BF16), kn.T.astype(BF16),
                     preferred_element_type=F32)
        diff = gt["gc_col"][:, _L_A + hd:_L_A + hd + 1] - gt["gc_row"][hd:hd + 1, :]
        a = kq[:BLK] * jnp.exp(diff + nm_strict)
        qk_m[pr] = (kq[BLK:] * jnp.exp(diff + nm_lower)).astype(BF16)
        ab = a.astype(BF16)
        p_m[pr] = eye - a
        x_m[pr] = jnp.dot(ab, ab, preferred_element_type=F32)

    def inv_iter(sb, j):
        for hd in range(N_HEADS):
            pr = (sb, hd)
            xb = x_m[pr].astype(BF16)
            if j + 1 < n_fac:
                px = jnp.dot(jnp.concatenate([p_m[pr].astype(BF16), xb], axis=0), xb,
                             preferred_element_type=F32)
                p_m[pr] = p_m[pr] + px[:BLK]
                x_m[pr] = px[BLK:]
            else:
                p_m[pr] = p_m[pr] + jnp.dot(p_m[pr].astype(BF16), xb, preferred_element_type=F32)

    def apply_inverse(sb):
        for hd in range(N_HEADS):
            pr = (sb, hd)
            uw = jnp.dot(p_m[pr].astype(BF16), rhs_uw[pr], preferred_element_type=F32)
            u_m[pr] = uw[:, :HEAD_DIM]
            w_m[pr] = uw[:, HEAD_DIM:].astype(BF16)

    state = [s_ref[hd] for hd in range(N_HEADS)]
    vnew_parts, qs_parts = {}, {}

    def state_chunk(sb, cb):
        rows = slice(cb * C, (cb + 1) * C)
        for hd in range(N_HEADS):
            pr = (sb, hd)
            rr = jnp.dot(jnp.concatenate([w_m[pr][rows], qg_m[pr][rows]], axis=0),
                         state[hd].astype(BF16), preferred_element_type=F32)
            vnew = u_m[pr][rows] - rr[:C]
            qs_parts.setdefault(pr, []).append(rr[C:])
            vnew_parts.setdefault(pr, []).append(vnew)
            upd = lax.dot_general(kd_m[pr][rows], vnew.astype(BF16), (((0,), (0,)), ((), ())),
                                  preferred_element_type=F32)
            state[hd] = state[hd] * etot_m[pr][cb * C:cb * C + 1, :] + upd

    def out_head(sb, hd):
        pr = (sb, hd)
        rows = slice(sb * BLK, (sb + 1) * BLK)
        sl = slice(hd * HEAD_DIM, (hd + 1) * HEAD_DIM)
        vn = jnp.concatenate(vnew_parts[pr], axis=0).astype(BF16)
        o = jnp.concatenate(qs_parts[pr], axis=0) + jnp.dot(qk_m[pr], vn, preferred_element_type=F32)
        for g in range(BLK // SUBLANES):
            ch, j = divmod(g, C // SUBLANES)
            obuf[hd, pl.ds(ch * C + j, SUBLANES, stride=SUBLANES), :] = o[g * SUBLANES:(g + 1) * SUBLANES, :]
        o = obuf[hd]
        on = o * lax.rsqrt(jnp.mean(o * o, axis=-1, keepdims=True) + EPS) * gout_ref[...]
        o_ref[rows, sl] = (on * _silu(za_ref[rows, sl].astype(F32))).astype(BF16)

    inv_share = [[] for _ in range(per_blk)]
    for j in range(n_fac):
        inv_share[max(0, j - (n_fac - per_blk))].append(j)
    for t in range(-2, n_blk + 1):
        if 0 <= t + 2 < n_blk:
            prep_gates(t + 2)
        for slot in range(per_blk):
            if 0 <= t < n_blk:
                state_chunk(t, slot)
            if 0 <= t + 1 < n_blk:
                for j in inv_share[slot]:
                    inv_iter(t + 1, j)
                if slot == per_blk - 1:
                    apply_inverse(t + 1)
            if 0 <= t + 2 < n_blk:
                prep_head(t + 2, slot)
            if 0 <= t - 1 < n_blk:
                out_head(t - 1, slot)
    for hd in range(N_HEADS):
        s_ref[hd] = state[hd]


def _gdn(qkva, slab, gates_t, a_log, dt_bias, za, g_out, tt):
    B, T, _ = qkva.shape
    tok = lambda width: pl.BlockSpec((None, tt, width), lambda b, i: (b, i, 0))
    const = lambda shape: pl.BlockSpec(shape, lambda b, i: (0,) * len(shape))
    lane_vec = lambda v: jnp.zeros((1, LANES), F32).at[0, _L_A:_L_A + N_HEADS].set(v)
    sub_vec = lambda v: jnp.zeros((SUBLANES, 1), F32).at[0:N_HEADS, 0].set(v)
    masks = _gdn_masks()
    return pl.pallas_call(
        functools.partial(_gdn_kernel, tt=tt),
        grid=(B, T // tt),
        in_specs=[tok(3 * W_MIX), tok(LANES),
                  pl.BlockSpec((None, SUBLANES, tt), lambda b, i: (b, 0, i)),
                  const((1, LANES)), const((1, LANES)), const((SUBLANES, 1)), const((SUBLANES, 1))]
                 + [const(m.shape) for m in masks]
                 + [tok(W_MIX), const((1, HEAD_DIM))],
        out_specs=tok(W_MIX),
        out_shape=jax.ShapeDtypeStruct((B, T, W_MIX), BF16),
        scratch_shapes=[pltpu.VMEM((N_HEADS, HEAD_DIM, HEAD_DIM), F32),
                        pltpu.VMEM((N_HEADS, GDN_BLOCK, HEAD_DIM), F32)],
        compiler_params=pltpu.CompilerParams(dimension_semantics=("parallel", "arbitrary"),
                                             vmem_limit_bytes=VMEM_LIMIT),
        name="gdn",
    )(qkva, slab, gates_t, lane_vec(a_log), lane_vec(dt_bias), sub_vec(a_log), sub_vec(dt_bias),
      *masks, za, g_out.reshape(1, HEAD_DIM))


def _fox_kernel(fp_ref, qt_ref, k_ref, vt_ref, zf_ref, o_ref, m_ref, acc_ref, sa_ref, sb_ref, *, tq, tk, n_q):
    first_pair_base = (pl.program_id(0) * N_HEADS + pl.program_id(1)) * n_q
    ones_rows = jnp.ones((FOX_SUM_ROWS, tk), BF16)
    halves = (slice(0, tk), slice(tk, 2 * tk))

    def scores_into(buf, kb, qi, which=(0, 1)):
        k = k_ref[pl.ds(pl.multiple_of(kb * tk, tk), tk), :]
        for h in which:
            buf[h] = jnp.dot(k, qt_ref[2 * qi + h], preferred_element_type=F32)

    def consume(buf, kb, modes):
        vt = jnp.concatenate([vt_ref[kb], ones_rows], axis=0)
        live = [h for h in (0, 1) if modes[h] != "skip"]
        pts, alphas = {}, {}
        for h in live:
            st = buf[h]
            if modes[h] == "diag":
                krow = lax.broadcasted_iota(jnp.int32, (tk, tk), 0)
                qcol = lax.broadcasted_iota(jnp.int32, (tk, tk), 1)
                st = jnp.where(qcol >= krow, st, NEG_INF)
            m_prev = m_ref[:, halves[h]]
            m_new = jnp.maximum(m_prev, jnp.max(st, axis=0, keepdims=True))
            alphas[h] = jnp.exp2(m_prev - m_new)
            pts[h] = jnp.exp2(st - m_new).astype(BF16)
            m_ref[:, halves[h]] = m_new
        for h in live:
            acc_ref[:, halves[h]] = (alphas[h] * acc_ref[:, halves[h]]
                                     + jnp.dot(vt, pts[h], preferred_element_type=F32))

    full = ("full", "full")
    p0 = fp_ref[first_pair_base]
    scores_into(sa_ref, 2 * p0, 0)
    for qi in range(n_q):
        m_ref[...] = jnp.full_like(m_ref, NEG_INF)
        acc_ref[...] = jnp.zeros_like(acc_ref)

        def pair(p, carry, qi=qi):
            kb = 2 * p
            scores_into(sb_ref, kb + 1, qi)
            consume(sa_ref, kb, full)
            scores_into(sa_ref, kb + 2, qi)
            consume(sb_ref, kb + 1, full)
            return carry

        lax.fori_loop(p0, qi, pair, 0)
        scores_into(sb_ref, 2 * qi + 1, qi, which=(1,))
        consume(sa_ref, 2 * qi, ("diag", "full"))
        if qi + 1 < n_q:
            p0 = fp_ref[first_pair_base + qi + 1]
            scores_into(sa_ref, 2 * p0, qi + 1)
        consume(sb_ref, 2 * qi + 1, ("skip", "diag"))

        rows = slice(qi * tq, (qi + 1) * tq)
        ot = acc_ref[0:HEAD_DIM, :] * (1.0 / acc_ref[HEAD_DIM:HEAD_DIM + 1, :])
        o_ref[rows, :] = (ot.T * _silu(zf_ref[rows, :].astype(F32))).astype(BF16)


def _fox_first_pair(stats, tk, n_q):
    c_first = stats[:, :, 0, _L_F:_L_F + N_HEADS]
    c_last = stats[:, :, 1, _L_F:_L_F + N_HEADS]
    qk_bound = jnp.sqrt(jnp.max(stats[:, :, 2, :N_HEADS], axis=1) * jnp.max(stats[:, :, 3, :N_HEADS], axis=1))
    n_kb = stats.shape[1]
    upper = (2.0 * qk_bound[:, None, None, :] + c_first[:, 0::2, None, :][:, :n_q]
             - c_last[:, None, :, :])
    kb = jnp.arange(n_kb, dtype=jnp.int32)[None, None, :, None]
    diag = 2 * jnp.arange(n_q, dtype=jnp.int32)[None, :, None, None]
    needed = jnp.logical_or(upper >= -FOX_SKIP_LOG2, kb >= diag)
    first_block = jnp.min(jnp.where(needed, kb, n_kb), axis=2)
    return jnp.right_shift(first_block, 1).transpose(0, 2, 1).reshape(-1)


def _fox(qf_aug, kf_aug, vft, zf, stats, tk):
    B, _, T, _ = zf.shape
    tq = 2 * tk
    assert T % tq == 0
    n_q = T // tq
    first_pair = _fox_first_pair(stats, tk, n_q)
    grid_spec = pltpu.PrefetchScalarGridSpec(
        num_scalar_prefetch=1,
        grid=(B, N_HEADS),
        in_specs=[pl.BlockSpec((None, None, T // tk, 2 * HEAD_DIM, tk), lambda b, h, fp: (b, h, 0, 0, 0)),
                  pl.BlockSpec((None, None, T, 2 * HEAD_DIM), lambda b, h, fp: (b, h, 0, 0)),
                  pl.BlockSpec((None, None, T // tk, HEAD_DIM, tk), lambda b, h, fp: (b, h, 0, 0, 0)),
                  pl.BlockSpec((None, None, T, HEAD_DIM), lambda b, h, fp: (b, h, 0, 0))],
        out_specs=pl.BlockSpec((None, None, T, HEAD_DIM), lambda b, h, fp: (b, h, 0, 0)),
        scratch_shapes=[pltpu.VMEM((1, tq), F32), pltpu.VMEM((HEAD_DIM + FOX_SUM_ROWS, tq), F32),
                        pltpu.VMEM((2, tk, tk), F32), pltpu.VMEM((2, tk, tk), F32)])
    return pl.pallas_call(
        functools.partial(_fox_kernel, tq=tq, tk=tk, n_q=n_q),
        grid_spec=grid_spec,
        out_shape=jax.ShapeDtypeStruct((B, N_HEADS, T, HEAD_DIM), BF16),
        compiler_params=pltpu.CompilerParams(dimension_semantics=("parallel", "parallel"),
                                             vmem_limit_bytes=VMEM_LIMIT),
        name="fox",
    )(first_pair, qf_aug, kf_aug, vft, zf)


def _outproj_kernel(oa_ref, of_ref, ga_ref, gf_ref, x_ref, gate_ref, wa_ref, wf_ref, wo_ref, o_ref):
    ya = jnp.dot(oa_ref[...], wa_ref[...], preferred_element_type=F32)
    of = jnp.concatenate([of_ref[hd] for hd in range(N_HEADS)], axis=1)
    yf = jnp.dot(of, wf_ref[...], preferred_element_type=F32)
    merged = _sigmoid(ga_ref[...].astype(F32)) * ya + _sigmoid(gf_ref[...].astype(F32)) * yf
    res = jnp.dot(merged.astype(BF16), wo_ref[...], preferred_element_type=F32)
    o_ref[...] = x_ref[...] + gate_ref[...] * res


def _outproj(oa, of, ga, gf, x, gate, wa, wf, wo, tm):
    B, T, D = x.shape
    tok = lambda width: pl.BlockSpec((None, tm, width), lambda b, i: (b, i, 0))
    const = lambda shape: pl.BlockSpec(shape, lambda b, i: (0,) * len(shape))
    return pl.pallas_call(
        _outproj_kernel,
        grid=(B, T // tm),
        in_specs=[tok(W_MIX), pl.BlockSpec((None, N_HEADS, tm, HEAD_DIM), lambda b, i: (b, 0, i, 0)),
                  tok(D), tok(D), tok(D),
                  pl.BlockSpec((None, 1, D), lambda b, i: (b, 0, 0)),
                  const((W_MIX, D)), const((W_MIX, D)), const((D, D))],
        out_specs=tok(D),
        out_shape=jax.ShapeDtypeStruct((B, T, D), F32),
        compiler_params=pltpu.CompilerParams(dimension_semantics=("parallel", "parallel"),
                                             vmem_limit_bytes=VMEM_LIMIT),
        name="outproj",
    )(oa, of, ga, gf, x, gate, wa, wf, wo)


def _permute_w_in(w_in, d_model):
    o = 0
    cols = {}
    for name, width in (("qkva", 3 * W_MIX), ("a", N_HEADS), ("b", N_HEADS), ("za", W_MIX),
                        ("qf", W_MIX), ("kf", W_MIX), ("vf", W_MIX), ("f", N_HEADS), ("zf", W_MIX),
                        ("ga", d_model), ("gf", d_model)):
        cols[name] = w_in[:, o:o + width]
        o += width
    gates = jnp.concatenate([cols["a"], cols["b"], cols["f"],
                             jnp.zeros((w_in.shape[0], LANES - 3 * N_HEADS), w_in.dtype)], axis=1)
    return jnp.concatenate([cols["qkva"], cols["za"], cols["qf"], cols["kf"], cols["vf"], cols["zf"],
                            cols["ga"], cols["gf"], gates], axis=1).astype(BF16)


def kernel(x, c, w_ada, b_ada, g_norm, w_in, conv_w, A_log, dt_bias, g_gdn_out,
           g_q_fox, g_k_fox, b_f, w_o_gdn, w_o_fox, w_out):
    B, T, D = x.shape
    tm = tk = min(512, T)
    tt = min(4 * GDN_BLOCK, T)

    mod = _adaln(c, w_ada, b_ada)
    shift, scale, gate = (mod[:, j * D:(j + 1) * D].reshape(B, 1, D) for j in range(3))

    w_perm = _permute_w_in(w_in, D)
    bf_lane = jnp.zeros((1, LANES), F32).at[0, _L_F:_L_F + N_HEADS].set(b_f)
    (qkva, za, qf_aug, kf_aug, vft, zf, ga, gf, slab, stats) = _inproj(
        x, shift, scale, g_norm.reshape(1, D), w_perm, conv_w,
        g_q_fox.reshape(1, HEAD_DIM), g_k_fox.reshape(1, HEAD_DIM), bf_lane, tm)

    gates_t = jnp.transpose(slab[:, :, :SUBLANES], (0, 2, 1))
    oa = _gdn(qkva, slab, gates_t, A_log, dt_bias, za, g_gdn_out, tt)
    of = _fox(qf_aug, kf_aug, vft, zf, stats, tk)
    return _outproj(oa, of, ga, gf, x, gate, w_o_gdn.astype(BF16), w_o_fox.astype(BF16),
                    w_out.astype(BF16), min(2 * tm, T))
```

```python
import functools

import jax
import jax.numpy as jnp
from jax import lax
from jax.experimental import pallas as pl
from jax.experimental.pallas import tpu as pltpu

F32 = jnp.float32
BF16 = jnp.bfloat16
EPS = 1e-6
HEAD_DIM = 128
N_HEADS = 4
W_MIX = N_HEADS * HEAD_DIM
CONV_WIDTH = 4
GDN_CHUNK = 64
GDN_BLOCK = 256
MASK_BIG = 1e30
LANES = 128
MXU_TILE = 256
SUBLANES = 8
VMEM_LIMIT = 56 * 1024 * 1024
HI = lax.Precision.HIGHEST
NEG_INF = float("-inf")
LOG2E = 1.4426950408889634
FOX_SUM_ROWS = 16
FOX_SKIP_LOG2 = 152.0

_C_QKVA = 0
_C_ZA = 3 * W_MIX
_C_QF = _C_ZA + W_MIX
_C_KF = _C_QF + W_MIX
_C_VF = _C_KF + W_MIX
_C_ZF = _C_VF + W_MIX
_C_GA = _C_ZF + W_MIX
_L_A, _L_B, _L_F = 0, N_HEADS, 2 * N_HEADS


def _softplus(z):
    return jnp.maximum(z, 0.0) + jnp.log(1.0 + jnp.exp(-jnp.abs(z)))


def _log_sigmoid(z):
    return jnp.minimum(z, 0.0) - jnp.log(1.0 + jnp.exp(-jnp.abs(z)))


def _sigmoid(z):
    return 1.0 / (1.0 + jnp.exp(-z))


def _silu(z):
    return z * _sigmoid(z)


def _split3_bf16(c):
    hi = c.astype(BF16).astype(F32)
    r = c - hi
    mid = r.astype(BF16).astype(F32)
    lo = (r - mid).astype(BF16).astype(F32)
    return hi, mid, lo


def _adaln_kernel(c_ref, w_ref, b_ref, o_ref):
    o_ref[...] = jnp.dot(c_ref[...], w_ref[...], precision=HI, preferred_element_type=F32) + b_ref[...]


def _adaln(c, w_ada, b_ada):
    B, D = c.shape
    n = w_ada.shape[1]
    bn = D
    return pl.pallas_call(
        _adaln_kernel,
        grid=(n // bn,),
        in_specs=[pl.BlockSpec((B, D), lambda j: (0, 0)),
                  pl.BlockSpec((D, bn), lambda j: (0, j)),
                  pl.BlockSpec((1, bn), lambda j: (0, j))],
        out_specs=pl.BlockSpec((B, bn), lambda j: (0, j)),
        out_shape=jax.ShapeDtypeStruct((B, n), F32),
        compiler_params=pltpu.CompilerParams(dimension_semantics=("arbitrary",), vmem_limit_bytes=VMEM_LIMIT),
        name="adaln",
    )(c, w_ada, b_ada.reshape(1, n))


def _inproj_kernel(x_ref, shift_ref, scale_ref, gn_ref, w_ref, convw_ref, tri_ref,
                   gq_ref, gk_ref, bf_ref,
                   qkva_ref, za_ref, qf_ref, kf_ref, vft_ref, zf_ref, ga_ref, gf_ref, gates_ref, stats_ref,
                   carry_ref, xbuf_q, xbuf_k, xbuf_v, hbuf, gbuf, *, tm, d_model):
    i = pl.program_id(1)
    xbufs = (xbuf_q, xbuf_k, xbuf_v)

    @pl.when(i == 0)
    def _():
        carry_ref[...] = jnp.zeros_like(carry_ref)
        for xb in xbufs:
            xb[0:SUBLANES, :] = jnp.zeros((SUBLANES, W_MIX), F32)

    x = x_ref[...]
    ms = jnp.mean(x * x, axis=-1, keepdims=True)
    h = x * lax.rsqrt(ms + EPS) * gn_ref[...]
    h = h * (1.0 + scale_ref[...]) + shift_ref[...]
    hb = h.astype(BF16)

    def chunk_permuted(ref):
        return jnp.concatenate([ref[pl.ds(c * GDN_CHUNK + j, SUBLANES, stride=SUBLANES), :]
                                for c in range(tm // GDN_CHUNK) for j in range(SUBLANES)], axis=0)

    for kg in range(d_model // LANES):
        hbuf[kg] = h[:, kg * LANES:(kg + 1) * LANES]
    hbp = jnp.concatenate([chunk_permuted(hbuf.at[kg]) for kg in range(d_model // LANES)],
                          axis=1).astype(BF16)

    def proj(lo, width):
        return jnp.dot(hb, w_ref[:, lo:lo + width], preferred_element_type=F32)

    NT = MXU_TILE
    RC = tm // 4

    def qkva_tile(t):
        j, half = divmod(t, W_MIX // NT)
        xbufs[j][SUBLANES:SUBLANES + tm, half * NT:(half + 1) * NT] = jnp.dot(
            hbp, w_ref[:, _C_QKVA + t * NT:_C_QKVA + (t + 1) * NT], preferred_element_type=F32)

    def conv_silu(j, rc):
        xbuf = xbufs[j]
        cols = slice(j * W_MIX, (j + 1) * W_MIX)
        n_tail = CONV_WIDTH - 1
        for c in range(rc * RC // GDN_CHUNK, (rc + 1) * RC // GDN_CHUNK):
            b0 = SUBLANES + c * GDN_CHUNK
            wrapped = []
            for g in range(SUBLANES - n_tail, SUBLANES):
                prev = g if c == 0 else b0 - GDN_CHUNK + SUBLANES * g + SUBLANES - 1
                wrapped += [xbuf[prev:prev + 1, :], xbuf[b0 + SUBLANES * g:b0 + SUBLANES * (g + 1) - 1, :]]
            ext = jnp.concatenate(wrapped + [xbuf[b0:b0 + GDN_CHUNK, :]], axis=0)
            y = ext[0:GDN_CHUNK] * convw_ref[0:1, cols]
            for tap in range(1, CONV_WIDTH):
                y = y + ext[SUBLANES * tap:SUBLANES * tap + GDN_CHUNK] * convw_ref[tap:tap + 1, cols]
            qkva_ref[c * GDN_CHUNK:(c + 1) * GDN_CHUNK, cols] = _silu(y).astype(BF16)
        if rc == tm // RC - 1:
            last = SUBLANES + tm - GDN_CHUNK
            for g in range(SUBLANES - n_tail, SUBLANES):
                r = last + SUBLANES * g + SUBLANES - 1
                xbuf[g:g + 1, :] = xbuf[r:r + 1, :]

    def plain(out_ref, lo, t):
        out_ref[:, t * NT:(t + 1) * NT] = proj(lo + t * NT, NT).astype(BF16)

    heads_per_tile = NT // HEAD_DIM

    def vft_tile(t):
        vt = proj(_C_VF + t * NT, NT).astype(BF16).T
        for n in range(heads_per_tile):
            vft_ref[heads_per_tile * t + n] = vt[n * HEAD_DIM:(n + 1) * HEAD_DIM, :]

    def zf_tile(t):
        z = proj(_C_ZF + t * NT, NT).astype(BF16)
        for n in range(heads_per_tile):
            zf_ref[heads_per_tile * t + n] = z[:, n * HEAD_DIM:(n + 1) * HEAD_DIM]

    qkva_tile(0)
    qkva_tile(1)
    for t in (2, 3, 4, 5):
        qkva_tile(t)
        conv_silu(0, t - 2)
    g = proj(_C_GA + 2 * d_model, LANES)
    conv_silu(1, 0)
    qk_tiles = {}
    for n, (name, lo) in enumerate((("q", _C_QF), ("q", _C_QF + NT), ("k", _C_KF), ("k", _C_KF + NT))):
        qk_tiles[(name, n % 2)] = proj(lo, NT)
        if n < 3:
            conv_silu(1, n + 1)

    gbuf[...] = g
    gates_ref[...] = chunk_permuted(gbuf)
    logf = _log_sigmoid(g + bf_ref[...])
    hi, mid, lo = _split3_bf16(logf)
    lane_g = lax.broadcasted_iota(jnp.int32, (tm, LANES), 1)
    packed = jnp.where(lane_g < _L_F + N_HEADS, hi,
                       jnp.where(lane_g < _L_F + 2 * N_HEADS, pltpu.roll(mid, N_HEADS, 1),
                                 jnp.where(lane_g < _L_F + 3 * N_HEADS, pltpu.roll(lo, 2 * N_HEADS, 1), 0.0)))
    cs = jnp.dot(tri_ref[...], packed.astype(BF16), preferred_element_type=F32)
    csum = (cs + pltpu.roll(cs, LANES - N_HEADS, 1)) + pltpu.roll(cs, LANES - 2 * N_HEADS, 1) + carry_ref[...]
    carry_ref[...] = csum[tm - 1:tm, :]

    for n, m_step in enumerate((functools.partial(plain, za_ref, _C_ZA, 0),
                                functools.partial(plain, za_ref, _C_ZA, 1),
                                functools.partial(zf_tile, 0), functools.partial(zf_tile, 1))):
        m_step()
        conv_silu(2, n)

    HR = tm // 2
    lane = lax.broadcasted_iota(jnp.int32, (HR, LANES), 1)
    one = jnp.where(lane < 3, 0.0, jnp.where(lane < 6, 1.0, 0.0))
    one_k = jnp.where(lane < 3, 1.0, 0.0)

    def qk_epilogue(hd, half):
        rows = slice(half * HR, (half + 1) * HR)
        sl = slice((hd % 2) * HEAD_DIM, (hd % 2 + 1) * HEAD_DIM)
        qh = qk_tiles[("q", hd // 2)][rows, sl]
        kh = qk_tiles[("k", hd // 2)][rows, sl]
        qn = qh * lax.rsqrt(jnp.mean(qh * qh, axis=-1, keepdims=True) + EPS) * gq_ref[...]
        kn = kh * lax.rsqrt(jnp.mean(kh * kh, axis=-1, keepdims=True) + EPS) * gk_ref[...]
        ch = jnp.broadcast_to(csum[rows, _L_F + hd:_L_F + hd + 1], (HR, LANES)) * LOG2E
        hi, mid, lo = _split3_bf16(ch)
        aug_q = jnp.where(lane == 0, hi, jnp.where(lane == 1, mid, jnp.where(lane == 2, lo, one)))
        aug_k = jnp.where(lane == 3, -hi, jnp.where(lane == 4, -mid, jnp.where(lane == 5, -lo, one_k)))
        qb = (qn * (HEAD_DIM ** -0.5 * LOG2E)).astype(BF16)
        kb = kn.astype(BF16)
        qf_ref[hd, 0:HEAD_DIM, rows] = qb.T
        qf_ref[hd, HEAD_DIM:2 * HEAD_DIM, rows] = aug_q.astype(BF16).T
        kf_ref[hd, rows, 0:HEAD_DIM] = kb
        kf_ref[hd, rows, HEAD_DIM:2 * HEAD_DIM] = aug_k.astype(BF16)
        for name, val in (("q", qb), ("k", kb)):
            v32 = val.astype(F32)
            sq_max[(name, hd, half)] = jnp.max(jnp.sum(v32 * v32, axis=-1, keepdims=True),
                                               axis=0, keepdims=True)

    sq_max = {}
    m_steps = ([functools.partial(vft_tile, t) for t in range(W_MIX // NT)]
               + [functools.partial(plain, ga_ref, _C_GA, t) for t in range(d_model // NT)]
               + [functools.partial(plain, gf_ref, _C_GA + d_model, t) for t in range(d_model // NT)])
    v_steps = [functools.partial(qk_epilogue, hd, half) for hd in range(N_HEADS) for half in range(2)]
    for n, m_step in enumerate(m_steps):
        m_step()
        if n < len(v_steps):
            v_steps[n]()

    lane1 = lax.broadcasted_iota(jnp.int32, (1, LANES), 1)
    norm_rows = []
    for name in ("q", "k"):
        row = jnp.zeros((1, LANES), F32)
        for hd in range(N_HEADS):
            row = jnp.where(lane1 == hd, jnp.maximum(sq_max[(name, hd, 0)], sq_max[(name, hd, 1)]), row)
        norm_rows.append(row)
    stats_ref[...] = jnp.concatenate(
        [csum[0:1, :] * LOG2E, csum[tm - 1:tm, :] * LOG2E] + norm_rows
        + [jnp.zeros((SUBLANES - 4, LANES), F32)], axis=0)


def _inproj(x, shift, scale, g_norm, w_perm, conv_w, g_q, g_k, bf_lane, tm):
    B, T, D = x.shape
    n_w = w_perm.shape[1]
    tri = jnp.tril(jnp.ones((tm, tm), BF16))
    tok = lambda width: pl.BlockSpec((None, tm, width), lambda b, i: (b, i, 0))
    per_b = pl.BlockSpec((None, 1, D), lambda b, i: (b, 0, 0))
    const = lambda shape: pl.BlockSpec(shape, lambda b, i: (0,) * len(shape))
    bf16_out = lambda w: (tok(w), jax.ShapeDtypeStruct((B, T, w), BF16))
    head_tok = lambda w: (pl.BlockSpec((None, N_HEADS, tm, w), lambda b, i: (b, 0, i, 0)),
                          jax.ShapeDtypeStruct((B, N_HEADS, T, w), BF16))
    head_feat = lambda w: (pl.BlockSpec((None, N_HEADS, None, w, tm), lambda b, i: (b, 0, i, 0, 0)),
                           jax.ShapeDtypeStruct((B, N_HEADS, T // tm, w, tm), BF16))
    outs = [bf16_out(3 * W_MIX), bf16_out(W_MIX), head_feat(2 * HEAD_DIM), head_tok(2 * HEAD_DIM),
            head_feat(HEAD_DIM),
            head_tok(HEAD_DIM), bf16_out(D), bf16_out(D),
            (tok(LANES), jax.ShapeDtypeStruct((B, T, LANES), F32)),
            (pl.BlockSpec((None, None, SUBLANES, LANES), lambda b, i: (b, i, 0, 0)),
             jax.ShapeDtypeStruct((B, T // tm, SUBLANES, LANES), F32))]
    return pl.pallas_call(
        functools.partial(_inproj_kernel, tm=tm, d_model=D),
        grid=(B, T // tm),
        in_specs=[tok(D), per_b, per_b, const((1, D)), const((D, n_w)),
                  const((CONV_WIDTH, 3 * W_MIX)), const((tm, tm)),
                  const((1, HEAD_DIM)), const((1, HEAD_DIM)), const((1, LANES))],
        out_specs=[o[0] for o in outs],
        out_shape=[o[1] for o in outs],
        scratch_shapes=([pltpu.VMEM((1, LANES), F32)] + [pltpu.VMEM((tm + SUBLANES, W_MIX), F32)] * 3
                        + [pltpu.VMEM((D // LANES, tm, LANES), F32), pltpu.VMEM((tm, LANES), F32)]),
        compiler_params=pltpu.CompilerParams(dimension_semantics=("parallel", "arbitrary"),
                                             vmem_limit_bytes=VMEM_LIMIT),
        name="inproj",
    )(x, shift, scale, g_norm, w_perm, conv_w, tri, g_q, g_k, bf_lane)


def _gdn_masks():
    def token(n):
        within = n % GDN_CHUNK
        return n - within + SUBLANES * (within % SUBLANES) + within // SUBLANES

    row = lax.broadcasted_iota(jnp.int32, (GDN_BLOCK, GDN_BLOCK), 0)
    col_ = lax.broadcasted_iota(jnp.int32, (GDN_BLOCK, GDN_BLOCK), 1)
    r, c = token(row), token(col_)
    same = (r // GDN_CHUNK) == (c // GDN_CHUNK)
    lower = jnp.logical_and(same, r >= c)
    f_lower = jnp.where(lower, 1.0, 0.0)
    f_same = jnp.where(same, 1.0, 0.0)
    f_cum = jnp.concatenate([f_lower, f_same], axis=0).astype(BF16)
    f_upper = jnp.where(jnp.logical_and(same, r <= c), 1.0, 0.0).astype(BF16)
    eye = jnp.where(r == c, 1.0, 0.0).astype(F32)
    nm_lower = jnp.where(lower, 0.0, -MASK_BIG).astype(F32)
    nm_strict = jnp.where(jnp.logical_and(same, r > c), 0.0, -MASK_BIG).astype(F32)
    return f_cum, f_upper, eye, nm_lower, nm_strict


def _gdn_kernel(qkv_ref, slab_ref, gt_ref, acol_ref, dcol_ref, arow_ref, drow_ref,
                fcum_ref, fupper_ref, eye_ref, nml_ref, nms_ref,
                za_ref, gout_ref, o_ref, s_ref, obuf, *, tt):
    i = pl.program_id(1)
    C = GDN_CHUNK
    BLK = GDN_BLOCK
    shift = C.bit_length() - 1

    @pl.when(i == 0)
    def _():
        s_ref[...] = jnp.zeros_like(s_ref)

    f_cum = fcum_ref[...]
    f_upper = fupper_ref[...]
    eye = eye_ref[...]
    nm_lower = nml_ref[...]
    nm_strict = nms_ref[...]
    n_blk = tt // BLK
    per_blk = BLK // C
    n_fac = shift - 1
    assert per_blk == N_HEADS

    def split2(v):
        hi = v.astype(BF16)
        return hi, (v - hi.astype(F32)).astype(BF16)

    def col(slab_, lane):
        return jnp.broadcast_to(slab_[:, lane:lane + 1], (slab_.shape[0], HEAD_DIM))

    gates = {}

    def prep_gates(sb):
        rows = slice(sb * BLK, (sb + 1) * BLK)
        slab = slab_ref[rows, :]
        g_col = -jnp.exp(acol_ref[...]) * _softplus(slab + dcol_ref[...])
        g_row = -jnp.exp(arow_ref[...]) * _softplus(gt_ref[:, rows] + drow_ref[...])
        g_hi, g_lo = split2(g_col)
        cum = jnp.dot(f_cum, jnp.concatenate([g_hi, g_lo], axis=1), preferred_element_type=F32)
        cum = cum[:, :LANES] + cum[:, LANES:]
        gc_col, gl_col = cum[:BLK], cum[BLK:]
        r_hi, r_lo = split2(g_row)
        cr = jnp.dot(jnp.concatenate([r_hi, r_lo], axis=0), f_upper, preferred_element_type=F32)
        gates[sb] = dict(beta=_sigmoid(slab), gc_col=gc_col, gc_row=cr[:SUBLANES] + cr[SUBLANES:],
                         e_gc=jnp.exp(gc_col), e_rem=jnp.exp(gl_col - gc_col), e_tot=jnp.exp(gl_col))

    qk_m, rhs_uw, qg_m, kd_m, etot_m, p_m, x_m, u_m, w_m = {}, {}, {}, {}, {}, {}, {}, {}, {}

    def prep_head(sb, hd):
        pr = (sb, hd)
        gt = gates[sb]
        rows = slice(sb * BLK, (sb + 1) * BLK)
        qh = qkv_ref[rows, hd * HEAD_DIM:(hd + 1) * HEAD_DIM].astype(F32)
        kh = qkv_ref[rows, W_MIX + hd * HEAD_DIM:W_MIX + (hd + 1) * HEAD_DIM].astype(F32)
        vh = qkv_ref[rows, 2 * W_MIX + hd * HEAD_DIM:2 * W_MIX + (hd + 1) * HEAD_DIM].astype(F32)
        qn = qh * lax.rsqrt(jnp.sum(qh * qh, axis=-1, keepdims=True) + EPS) * (HEAD_DIM ** -0.5)
        kn = kh * lax.rsqrt(jnp.sum(kh * kh, axis=-1, keepdims=True) + EPS)
        bh = col(gt["beta"], _L_B + hd)
        e_gc = col(gt["e_gc"], _L_A + hd)
        kb = kn * bh
        rhs_uw[pr] = jnp.concatenate([vh * bh, kb * e_gc], axis=1).astype(BF16)
        qg_m[pr] = (qn * e_gc).astype(BF16)
        kd_m[pr] = (kn * col(gt["e_rem"], _L_A + hd)).astype(BF16)
        etot_m[pr] = col(gt["e_tot"], _L_A + hd)
        kq = jnp.dot(jnp.concatenate([kb, qn], axis=0).astype(BF16), kn.T.astype(BF16),
                     preferred_element_type=F32)
        diff = gt["gc_col"][:, _L_A + hd:_L_A + hd + 1] - gt["gc_row"][hd:hd + 1, :]
        a = kq[:BLK] * jnp.exp(diff + nm_strict)
        qk_m[pr] = (kq[BLK:] * jnp.exp(diff + nm_lower)).astype(BF16)
        ab = a.astype(BF16)
        p_m[pr] = eye - a
        x_m[pr] = jnp.dot(ab, ab, preferred_element_type=F32)

    def inv_iter(sb, j):
        for hd in range(N_HEADS):
            pr = (sb, hd)
            xb = x_m[pr].astype(BF16)
            if j + 1 < n_fac:
                px = jnp.dot(jnp.concatenate([p_m[pr].astype(BF16), xb], axis=0), xb,
                             preferred_element_type=F32)
                p_m[pr] = p_m[pr] + px[:BLK]
                x_m[pr] = px[BLK:]
            else:
                p_m[pr] = p_m[pr] + jnp.dot(p_m[pr].astype(BF16), xb, preferred_element_type=F32)

    def apply_inverse(sb):
        for hd in range(N_HEADS):
            pr = (sb, hd)
            uw = jnp.dot(p_m[pr].astype(BF16), rhs_uw[pr], preferred_element_type=F32)
            u_m[pr] = uw[:, :HEAD_DIM]
            w_m[pr] = uw[:, HEAD_DIM:].astype(BF16)

    state = [s_ref[hd] for hd in range(N_HEADS)]
    vnew_parts, qs_parts = {}, {}

    def state_chunk(sb, cb):
        rows = slice(cb * C, (cb + 1) * C)
        for hd in range(N_HEADS):
            pr = (sb, hd)
            rr = jnp.dot(jnp.concatenate([w_m[pr][rows], qg_m[pr][rows]], axis=0),
                         state[hd].astype(BF16), preferred_element_type=F32)
            vnew = u_m[pr][rows] - rr[:C]
            qs_parts.setdefault(pr, []).append(rr[C:])
            vnew_parts.setdefault(pr, []).append(vnew)
            upd = lax.dot_general(kd_m[pr][rows], vnew.astype(BF16), (((0,), (0,)), ((), ())),
                                  preferred_element_type=F32)
            state[hd] = state[hd] * etot_m[pr][cb * C:cb * C + 1, :] + upd

    def out_head(sb, hd):
        pr = (sb, hd)
        rows = slice(sb * BLK, (sb + 1) * BLK)
        sl = slice(hd * HEAD_DIM, (hd + 1) * HEAD_DIM)
        vn = jnp.concatenate(vnew_parts[pr], axis=0).astype(BF16)
        o = jnp.concatenate(qs_parts[pr], axis=0) + jnp.dot(qk_m[pr], vn, preferred_element_type=F32)
        for g in range(BLK // SUBLANES):
            ch, j = divmod(g, C // SUBLANES)
            obuf[hd, pl.ds(ch * C + j, SUBLANES, stride=SUBLANES), :] = o[g * SUBLANES:(g + 1) * SUBLANES, :]
        o = obuf[hd]
        on = o * lax.rsqrt(jnp.mean(o * o, axis=-1, keepdims=True) + EPS) * gout_ref[...]
        o_ref[rows, sl] = (on * _silu(za_ref[rows, sl].astype(F32))).astype(BF16)

    inv_share = [[] for _ in range(per_blk)]
    for j in range(n_fac):
        inv_share[max(0, j - (n_fac - per_blk))].append(j)
    for t in range(-2, n_blk + 1):
        if 0 <= t + 2 < n_blk:
            prep_gates(t + 2)
        for slot in range(per_blk):
            if 0 <= t < n_blk:
                state_chunk(t, slot)
            if 0 <= t + 1 < n_blk:
                for j in inv_share[slot]:
                    inv_iter(t + 1, j)
                if slot == per_blk - 1:
                    apply_inverse(t + 1)
            if 0 <= t + 2 < n_blk:
                prep_head(t + 2, slot)
            if 0 <= t - 1 < n_blk:
                out_head(t - 1, slot)
    for hd in range(N_HEADS):
        s_ref[hd] = state[hd]


def _gdn(qkva, slab, gates_t, a_log, dt_bias, za, g_out, tt):
    B, T, _ = qkva.shape
    tok = lambda width: pl.BlockSpec((None, tt, width), lambda b, i: (b, i, 0))
    const = lambda shape: pl.BlockSpec(shape, lambda b, i: (0,) * len(shape))
    lane_vec = lambda v: jnp.zeros((1, LANES), F32).at[0, _L_A:_L_A + N_HEADS].set(v)
    sub_vec = lambda v: jnp.zeros((SUBLANES, 1), F32).at[0:N_HEADS, 0].set(v)
    masks = _gdn_masks()
    return pl.pallas_call(
        functools.partial(_gdn_kernel, tt=tt),
        grid=(B, T // tt),
        in_specs=[tok(3 * W_MIX), tok(LANES),
                  pl.BlockSpec((None, SUBLANES, tt), lambda b, i: (b, 0, i)),
                  const((1, LANES)), const((1, LANES)), const((SUBLANES, 1)), const((SUBLANES, 1))]
                 + [const(m.shape) for m in masks]
                 + [tok(W_MIX), const((1, HEAD_DIM))],
        out_specs=tok(W_MIX),
        out_shape=jax.ShapeDtypeStruct((B, T, W_MIX), BF16),
        scratch_shapes=[pltpu.VMEM((N_HEADS, HEAD_DIM, HEAD_DIM), F32),
                        pltpu.VMEM((N_HEADS, GDN_BLOCK, HEAD_DIM), F32)],
        compiler_params=pltpu.CompilerParams(dimension_semantics=("parallel", "arbitrary"),
                                             vmem_limit_bytes=VMEM_LIMIT),
        name="gdn",
    )(qkva, slab, gates_t, lane_vec(a_log), lane_vec(dt_bias), sub_vec(a_log), sub_vec(dt_bias),
      *masks, za, g_out.reshape(1, HEAD_DIM))


def _fox_kernel(fp_ref, qt_ref, k_ref, vt_ref, zf_ref, o_ref, m_ref, acc_ref, sa_ref, sb_ref, *, tq, tk, n_q):
    first_pair_base = (pl.program_id(0) * N_HEADS + pl.program_id(1)) * n_q
    ones_rows = jnp.ones((FOX_SUM_ROWS, tk), BF16)
    halves = (slice(0, tk), slice(tk, 2 * tk))

    def scores_into(buf, kb, qi, which=(0, 1)):
        k = k_ref[pl.ds(pl.multiple_of(kb * tk, tk), tk), :]
        for h in which:
            buf[h] = jnp.dot(k, qt_ref[2 * qi + h], preferred_element_type=F32)

    def consume(buf, kb, modes):
        vt = jnp.concatenate([vt_ref[kb], ones_rows], axis=0)
        live = [h for h in (0, 1) if modes[h] != "skip"]
        pts, alphas = {}, {}
        for h in live:
            st = buf[h]
            if modes[h] == "diag":
                krow = lax.broadcasted_iota(jnp.int32, (tk, tk), 0)
                qcol = lax.broadcasted_iota(jnp.int32, (tk, tk), 1)
                st = jnp.where(qcol >= krow, st, NEG_INF)
            m_prev = m_ref[:, halves[h]]
            m_new = jnp.maximum(m_prev, jnp.max(st, axis=0, keepdims=True))
            alphas[h] = jnp.exp2(m_prev - m_new)
            pts[h] = jnp.exp2(st - m_new).astype(BF16)
            m_ref[:, halves[h]] = m_new
        for h in live:
            acc_ref[:, halves[h]] = (alphas[h] * acc_ref[:, halves[h]]
                                     + jnp.dot(vt, pts[h], preferred_element_type=F32))

    full = ("full", "full")
    p0 = fp_ref[first_pair_base]
    scores_into(sa_ref, 2 * p0, 0)
    for qi in range(n_q):
        m_ref[...] = jnp.full_like(m_ref, NEG_INF)
        acc_ref[...] = jnp.zeros_like(acc_ref)

        def pair(p, carry, qi=qi):
            kb = 2 * p
            scores_into(sb_ref, kb + 1, qi)
            consume(sa_ref, kb, full)
            scores_into(sa_ref, kb + 2, qi)
            consume(sb_ref, kb + 1, full)
            return carry

        lax.fori_loop(p0, qi, pair, 0)
        scores_into(sb_ref, 2 * qi + 1, qi, which=(1,))
        consume(sa_ref, 2 * qi, ("diag", "full"))
        if qi + 1 < n_q:
            p0 = fp_ref[first_pair_base + qi + 1]
            scores_into(sa_ref, 2 * p0, qi + 1)
        consume(sb_ref, 2 * qi + 1, ("skip", "diag"))

        rows = slice(qi * tq, (qi + 1) * tq)
        ot = acc_ref[0:HEAD_DIM, :] * (1.0 / acc_ref[HEAD_DIM:HEAD_DIM + 1, :])
        o_ref[rows, :] = (ot.T * _silu(zf_ref[rows, :].astype(F32))).astype(BF16)


def _fox_first_pair(stats, tk, n_q):
    c_first = stats[:, :, 0, _L_F:_L_F + N_HEADS]
    c_last = stats[:, :, 1, _L_F:_L_F + N_HEADS]
    qk_bound = jnp.sqrt(jnp.max(stats[:, :, 2, :N_HEADS], axis=1) * jnp.max(stats[:, :, 3, :N_HEADS], axis=1))
    n_kb = stats.shape[1]
    upper = (2.0 * qk_bound[:, None, None, :] + c_first[:, 0::2, None, :][:, :n_q]
             - c_last[:, None, :, :])
    kb = jnp.arange(n_kb, dtype=jnp.int32)[None, None, :, None]
    diag = 2 * jnp.arange(n_q, dtype=jnp.int32)[None, :, None, None]
    needed = jnp.logical_or(upper >= -FOX_SKIP_LOG2, kb >= diag)
    first_block = jnp.min(jnp.where(needed, kb, n_kb), axis=2)
    return jnp.right_shift(first_block, 1).transpose(0, 2, 1).reshape(-1)


def _fox(qf_aug, kf_aug, vft, zf, stats, tk):
    B, _, T, _ = zf.shape
    tq = 2 * tk
    assert T % tq == 0
    n_q = T // tq
    first_pair = _fox_first_pair(stats, tk, n_q)
    grid_spec = pltpu.PrefetchScalarGridSpec(
        num_scalar_prefetch=1,
        grid=(B, N_HEADS),
        in_specs=[pl.BlockSpec((None, None, T // tk, 2 * HEAD_DIM, tk), lambda b, h, fp: (b, h, 0, 0, 0)),
                  pl.BlockSpec((None, None, T, 2 * HEAD_DIM), lambda b, h, fp: (b, h, 0, 0)),
                  pl.BlockSpec((None, None, T // tk, HEAD_DIM, tk), lambda b, h, fp: (b, h, 0, 0, 0)),
                  pl.BlockSpec((None, None, T, HEAD_DIM), lambda b, h, fp: (b, h, 0, 0))],
        out_specs=pl.BlockSpec((None, None, T, HEAD_DIM), lambda b, h, fp: (b, h, 0, 0)),
        scratch_shapes=[pltpu.VMEM((1, tq), F32), pltpu.VMEM((HEAD_DIM + FOX_SUM_ROWS, tq), F32),
                        pltpu.VMEM((2, tk, tk), F32), pltpu.VMEM((2, tk, tk), F32)])
    return pl.pallas_call(
        functools.partial(_fox_kernel, tq=tq, tk=tk, n_q=n_q),
        grid_spec=grid_spec,
        out_shape=jax.ShapeDtypeStruct((B, N_HEADS, T, HEAD_DIM), BF16),
        compiler_params=pltpu.CompilerParams(dimension_semantics=("parallel", "parallel"),
                                             vmem_limit_bytes=VMEM_LIMIT),
        name="fox",
    )(first_pair, qf_aug, kf_aug, vft, zf)


def _outproj_kernel(oa_ref, of_ref, ga_ref, gf_ref, x_ref, gate_ref, wa_ref, wf_ref, wo_ref, o_ref):
    ya = jnp.dot(oa_ref[...], wa_ref[...], preferred_element_type=F32)
    of = jnp.concatenate([of_ref[hd] for hd in range(N_HEADS)], axis=1)
    yf = jnp.dot(of, wf_ref[...], preferred_element_type=F32)
    merged = _sigmoid(ga_ref[...].astype(F32)) * ya + _sigmoid(gf_ref[...].astype(F32)) * yf
    res = jnp.dot(merged.astype(BF16), wo_ref[...], preferred_element_type=F32)
    o_ref[...] = x_ref[...] + gate_ref[...] * res


def _outproj(oa, of, ga, gf, x, gate, wa, wf, wo, tm):
    B, T, D = x.shape
    tok = lambda width: pl.BlockSpec((None, tm, width), lambda b, i: (b, i, 0))
    const = lambda shape: pl.BlockSpec(shape, lambda b, i: (0,) * len(shape))
    return pl.pallas_call(
        _outproj_kernel,
        grid=(B, T // tm),
        in_specs=[tok(W_MIX), pl.BlockSpec((None, N_HEADS, tm, HEAD_DIM), lambda b, i: (b, 0, i, 0)),
                  tok(D), tok(D), tok(D),
                  pl.BlockSpec((None, 1, D), lambda b, i: (b, 0, 0)),
                  const((W_MIX, D)), const((W_MIX, D)), const((D, D))],
        out_specs=tok(D),
        out_shape=jax.ShapeDtypeStruct((B, T, D), F32),
        compiler_params=pltpu.CompilerParams(dimension_semantics=("parallel", "parallel"),
                                             vmem_limit_bytes=VMEM_LIMIT),
        name="outproj",
    )(oa, of, ga, gf, x, gate, wa, wf, wo)


def _permute_w_in(w_in, d_model):
    o = 0
    cols = {}
    for name, width in (("qkva", 3 * W_MIX), ("a", N_HEADS), ("b", N_HEADS), ("za", W_MIX),
                        ("qf", W_MIX), ("kf", W_MIX), ("vf", W_MIX), ("f", N_HEADS), ("zf", W_MIX),
                        ("ga", d_model), ("gf", d_model)):
        cols[name] = w_in[:, o:o + width]
        o += width
    gates = jnp.concatenate([cols["a"], cols["b"], cols["f"],
                             jnp.zeros((w_in.shape[0], LANES - 3 * N_HEADS), w_in.dtype)], axis=1)
    return jnp.concatenate([cols["qkva"], cols["za"], cols["qf"], cols["kf"], cols["vf"], cols["zf"],
                            cols["ga"], cols["gf"], gates], axis=1).astype(BF16)


def kernel(x, c, w_ada, b_ada, g_norm, w_in, conv_w, A_log, dt_bias, g_gdn_out,
           g_q_fox, g_k_fox, b_f, w_o_gdn, w_o_fox, w_out):
    B, T, D = x.shape
    tm = tk = min(512, T)
    tt = min(4 * GDN_BLOCK, T)

    mod = _adaln(c, w_ada, b_ada)
    shift, scale, gate = (mod[:, j * D:(j + 1) * D].reshape(B, 1, D) for j in range(3))

    w_perm = _permute_w_in(w_in, D)
    bf_lane = jnp.zeros((1, LANES), F32).at[0, _L_F:_L_F + N_HEADS].set(b_f)
    (qkva, za, qf_aug, kf_aug, vft, zf, ga, gf, slab, stats) = _inproj(
        x, shift, scale, g_norm.reshape(1, D), w_perm, conv_w,
        g_q_fox.reshape(1, HEAD_DIM), g_k_fox.reshape(1, HEAD_DIM), bf_lane, tm)

    gates_t = jnp.transpose(slab[:, :, :SUBLANES], (0, 2, 1))
    oa = _gdn(qkva, slab, gates_t, A_log, dt_bias, za, g_gdn_out, tt)
    of = _fox(qf_aug, kf_aug, vft, zf, stats, tk)
    return _outproj(oa, of, ga, gf, x, gate, w_o_gdn.astype(BF16), w_o_fox.astype(BF16),
                    w_out.astype(BF16), min(2 * tm, T))
```

```python
import functools

import jax
import jax.numpy as jnp
from jax import lax
from jax.experimental import pallas as pl
from jax.experimental.pallas import tpu as pltpu

F32 = jnp.float32
BF16 = jnp.bfloat16
EPS = 1e-6
HEAD_DIM = 128
N_HEADS = 4
W_MIX = N_HEADS * HEAD_DIM
CONV_WIDTH = 4
GDN_CHUNK = 64
GDN_BLOCK = 256
MASK_BIG = 1e30
LANES = 128
MXU_TILE = 256
SUBLANES = 8
VMEM_LIMIT = 56 * 1024 * 1024
HI = lax.Precision.HIGHEST
NEG_INF = float("-inf")
LOG2E = 1.4426950408889634
FOX_SUM_ROWS = 16
FOX_SKIP_LOG2 = 152.0

_C_QKVA = 0
_C_ZA = 3 * W_MIX
_C_QF = _C_ZA + W_MIX
_C_KF = _C_QF + W_MIX
_C_VF = _C_KF + W_MIX
_C_ZF = _C_VF + W_MIX
_C_GA = _C_ZF + W_MIX
_L_A, _L_B, _L_F = 0, N_HEADS, 2 * N_HEADS


def _softplus(z):
    return jnp.maximum(z, 0.0) + jnp.log(1.0 + jnp.exp(-jnp.abs(z)))


def _log_sigmoid(z):
    return jnp.minimum(z, 0.0) - jnp.log(1.0 + jnp.exp(-jnp.abs(z)))


def _sigmoid(z):
    return 1.0 / (1.0 + jnp.exp(-z))


def _silu(z):
    return z * _sigmoid(z)


def _split3_bf16(c):
    hi = c.astype(BF16).astype(F32)
    r = c - hi
    mid = r.astype(BF16).astype(F32)
    lo = (r - mid).astype(BF16).astype(F32)
    return hi, mid, lo


def _adaln_kernel(c_ref, w_ref, b_ref, o_ref):
    o_ref[...] = jnp.dot(c_ref[...], w_ref[...], precision=HI, preferred_element_type=F32) + b_ref[...]


def _adaln(c, w_ada, b_ada):
    B, D = c.shape
    n = w_ada.shape[1]
    bn = D
    return pl.pallas_call(
        _adaln_kernel,
        grid=(n // bn,),
        in_specs=[pl.BlockSpec((B, D), lambda j: (0, 0)),
                  pl.BlockSpec((D, bn), lambda j: (0, j)),
                  pl.BlockSpec((1, bn), lambda j: (0, j))],
        out_specs=pl.BlockSpec((B, bn), lambda j: (0, j)),
        out_shape=jax.ShapeDtypeStruct((B, n), F32),
        compiler_params=pltpu.CompilerParams(dimension_semantics=("arbitrary",), vmem_limit_bytes=VMEM_LIMIT),
        name="adaln",
    )(c, w_ada, b_ada.reshape(1, n))


def _inproj_kernel(x_ref, shift_ref, scale_ref, gn_ref, w_ref, convw_ref, tri_ref,
                   gq_ref, gk_ref, bf_ref,
                   qkva_ref, za_ref, qf_ref, kf_ref, vft_ref, zf_ref, ga_ref, gf_ref, gates_ref, stats_ref,
                   carry_ref, xbuf_q, xbuf_k, xbuf_v, hbuf, gbuf, *, tm, d_model):
    i = pl.program_id(1)
    xbufs = (xbuf_q, xbuf_k, xbuf_v)

    @pl.when(i == 0)
    def _():
        carry_ref[...] = jnp.zeros_like(carry_ref)
        for xb in xbufs:
            xb[0:SUBLANES, :] = jnp.zeros((SUBLANES, W_MIX), F32)

    x = x_ref[...]
    ms = jnp.mean(x * x, axis=-1, keepdims=True)
    h = x * lax.rsqrt(ms + EPS) * gn_ref[...]
    h = h * (1.0 + scale_ref[...]) + shift_ref[...]
    hb = h.astype(BF16)

    def chunk_permuted(ref):
        return jnp.concatenate([ref[pl.ds(c * GDN_CHUNK + j, SUBLANES, stride=SUBLANES), :]
                                for c in range(tm // GDN_CHUNK) for j in range(SUBLANES)], axis=0)

    for kg in range(d_model // LANES):
        hbuf[kg] = h[:, kg * LANES:(kg + 1) * LANES]
    hbp = jnp.concatenate([chunk_permuted(hbuf.at[kg]) for kg in range(d_model // LANES)],
                          axis=1).astype(BF16)

    def proj(lo, width):
        return jnp.dot(hb, w_ref[:, lo:lo + width], preferred_element_type=F32)

    NT = MXU_TILE
    RC = tm // 4

    def qkva_tile(t):
        j, half = divmod(t, W_MIX // NT)
        xbufs[j][SUBLANES:SUBLANES + tm, half * NT:(half + 1) * NT] = jnp.dot(
            hbp, w_ref[:, _C_QKVA + t * NT:_C_QKVA + (t + 1) * NT], preferred_element_type=F32)

    def conv_silu(j, rc):
        xbuf = xbufs[j]
        cols = slice(j * W_MIX, (j + 1) * W_MIX)
        n_tail = CONV_WIDTH - 1
        for c in range(rc * RC // GDN_CHUNK, (rc + 1) * RC // GDN_CHUNK):
            b0 = SUBLANES + c * GDN_CHUNK
            wrapped = []
            for g in range(SUBLANES - n_tail, SUBLANES):
                prev = g if c == 0 else b0 - GDN_CHUNK + SUBLANES * g + SUBLANES - 1
                wrapped += [xbuf[prev:prev + 1, :], xbuf[b0 + SUBLANES * g:b0 + SUBLANES * (g + 1) - 1, :]]
            ext = jnp.concatenate(wrapped + [xbuf[b0:b0 + GDN_CHUNK, :]], axis=0)
            y = ext[0:GDN_CHUNK] * convw_ref[0:1, cols]
            for tap in range(1, CONV_WIDTH):
                y = y + ext[SUBLANES * tap:SUBLANES * tap + GDN_CHUNK] * convw_ref[tap:tap + 1, cols]
            qkva_ref[c * GDN_CHUNK:(c + 1) * GDN_CHUNK, cols] = _silu(y).astype(BF16)
        if rc == tm // RC - 1:
            last = SUBLANES + tm - GDN_CHUNK
            for g in range(SUBLANES - n_tail, SUBLANES):
                r = last + SUBLANES * g + SUBLANES - 1
                xbuf[g:g + 1, :] = xbuf[r:r + 1, :]

    def plain(out_ref, lo, t):
        out_ref[:, t * NT:(t + 1) * NT] = proj(lo + t * NT, NT).astype(BF16)

    heads_per_tile = NT // HEAD_DIM

    def vft_tile(t):
        vt = proj(_C_VF + t * NT, NT).astype(BF16).T
        for n in range(heads_per_tile):
            vft_ref[heads_per_tile * t + n] = vt[n * HEAD_DIM:(n + 1) * HEAD_DIM, :]

    def zf_tile(t):
        z = proj(_C_ZF + t * NT, NT).astype(BF16)
        for n in range(heads_per_tile):
            zf_ref[heads_per_tile * t + n] = z[:, n * HEAD_DIM:(n + 1) * HEAD_DIM]

    qkva_tile(0)
    qkva_tile(1)
    for t in (2, 3, 4, 5):
        qkva_tile(t)
        conv_silu(0, t - 2)
    g = proj(_C_GA + 2 * d_model, LANES)
    conv_silu(1, 0)
    qk_tiles = {}
    for n, (name, lo) in enumerate((("q", _C_QF), ("q", _C_QF + NT), ("k", _C_KF), ("k", _C_KF + NT))):
        qk_tiles[(name, n % 2)] = proj(lo, NT)
        if n < 3:
            conv_silu(1, n + 1)

    gbuf[...] = g
    gates_ref[...] = chunk_permuted(gbuf)
    logf = _log_sigmoid(g + bf_ref[...])
    hi, mid, lo = _split3_bf16(logf)
    lane_g = lax.broadcasted_iota(jnp.int32, (tm, LANES), 1)
    packed = jnp.where(lane_g < _L_F + N_HEADS, hi,
                       jnp.where(lane_g < _L_F + 2 * N_HEADS, pltpu.roll(mid, N_HEADS, 1),
                                 jnp.where(lane_g < _L_F + 3 * N_HEADS, pltpu.roll(lo, 2 * N_HEADS, 1), 0.0)))
    cs = jnp.dot(tri_ref[...], packed.astype(BF16), preferred_element_type=F32)
    csum = (cs + pltpu.roll(cs, LANES - N_HEADS, 1)) + pltpu.roll(cs, LANES - 2 * N_HEADS, 1) + carry_ref[...]
    carry_ref[...] = csum[tm - 1:tm, :]

    for n, m_step in enumerate((functools.partial(plain, za_ref, _C_ZA, 0),
                                functools.partial(plain, za_ref, _C_ZA, 1),
                                functools.partial(zf_tile, 0), functools.partial(zf_tile, 1))):
        m_step()
        conv_silu(2, n)

    HR = tm // 2
    lane = lax.broadcasted_iota(jnp.int32, (HR, LANES), 1)
    one = jnp.where(lane < 3, 0.0, jnp.where(lane < 6, 1.0, 0.0))
    one_k = jnp.where(lane < 3, 1.0, 0.0)

    def qk_epilogue(hd, half):
        rows = slice(half * HR, (half + 1) * HR)
        sl = slice((hd % 2) * HEAD_DIM, (hd % 2 + 1) * HEAD_DIM)
        qh = qk_tiles[("q", hd // 2)][rows, sl]
        kh = qk_tiles[("k", hd // 2)][rows, sl]
        qn = qh * lax.rsqrt(jnp.mean(qh * qh, axis=-1, keepdims=True) + EPS) * gq_ref[...]
        kn = kh * lax.rsqrt(jnp.mean(kh * kh, axis=-1, keepdims=True) + EPS) * gk_ref[...]
        ch = jnp.broadcast_to(csum[rows, _L_F + hd:_L_F + hd + 1], (HR, LANES)) * LOG2E
        hi, mid, lo = _split3_bf16(ch)
        aug_q = jnp.where(lane == 0, hi, jnp.where(lane == 1, mid, jnp.where(lane == 2, lo, one)))
        aug_k = jnp.where(lane == 3, -hi, jnp.where(lane == 4, -mid, jnp.where(lane == 5, -lo, one_k)))
        qb = (qn * (HEAD_DIM ** -0.5 * LOG2E)).astype(BF16)
        kb = kn.astype(BF16)
        qf_ref[hd, 0:HEAD_DIM, rows] = qb.T
        qf_ref[hd, HEAD_DIM:2 * HEAD_DIM, rows] = aug_q.astype(BF16).T
        kf_ref[hd, rows, 0:HEAD_DIM] = kb
        kf_ref[hd, rows, HEAD_DIM:2 * HEAD_DIM] = aug_k.astype(BF16)
        for name, val in (("q", qb), ("k", kb)):
            v32 = val.astype(F32)
            sq_max[(name, hd, half)] = jnp.max(jnp.sum(v32 * v32, axis=-1, keepdims=True),
                                               axis=0, keepdims=True)

    sq_max = {}
    m_steps = ([functools.partial(vft_tile, t) for t in range(W_MIX // NT)]
               + [functools.partial(plain, ga_ref, _C_GA, t) for t in range(d_model // NT)]
               + [functools.partial(plain, gf_ref, _C_GA + d_model, t) for t in range(d_model // NT)])
    v_steps = [functools.partial(qk_epilogue, hd, half) for hd in range(N_HEADS) for half in range(2)]
    for n, m_step in enumerate(m_steps):
        m_step()
        if n < len(v_steps):
            v_steps[n]()

    lane1 = lax.broadcasted_iota(jnp.int32, (1, LANES), 1)
    norm_rows = []
    for name in ("q", "k"):
        row = jnp.zeros((1, LANES), F32)
        for hd in range(N_HEADS):
            row = jnp.where(lane1 == hd, jnp.maximum(sq_max[(name, hd, 0)], sq_max[(name, hd, 1)]), row)
        norm_rows.append(row)
    stats_ref[...] = jnp.concatenate(
        [csum[0:1, :] * LOG2E, csum[tm - 1:tm, :] * LOG2E] + norm_rows
        + [jnp.zeros((SUBLANES - 4, LANES), F32)], axis=0)


def _inproj(x, shift, scale, g_norm, w_perm, conv_w, g_q, g_k, bf_lane, tm):
    B, T, D = x.shape
    n_w = w_perm.shape[1]
    tri = jnp.tril(jnp.ones((tm, tm), BF16))
    tok = lambda width: pl.BlockSpec((None, tm, width), lambda b, i: (b, i, 0))
    per_b = pl.BlockSpec((None, 1, D), lambda b, i: (b, 0, 0))
    const = lambda shape: pl.BlockSpec(shape, lambda b, i: (0,) * len(shape), pipeline_mode=pl.Buffered(1))
    bf16_out = lambda w: (tok(w), jax.ShapeDtypeStruct((B, T, w), BF16))
    head_tok = lambda w: (pl.BlockSpec((None, N_HEADS, tm, w), lambda b, i: (b, 0, i, 0)),
                          jax.ShapeDtypeStruct((B, N_HEADS, T, w), BF16))
    head_feat = lambda w: (pl.BlockSpec((None, N_HEADS, None, w, tm), lambda b, i: (b, 0, i, 0, 0)),
                           jax.ShapeDtypeStruct((B, N_HEADS, T // tm, w, tm), BF16))
    outs = [bf16_out(3 * W_MIX), bf16_out(W_MIX), head_feat(2 * HEAD_DIM), head_tok(2 * HEAD_DIM),
            head_feat(HEAD_DIM),
            head_tok(HEAD_DIM), bf16_out(D), bf16_out(D),
            (tok(LANES), jax.ShapeDtypeStruct((B, T, LANES), F32)),
            (pl.BlockSpec((None, None, SUBLANES, LANES), lambda b, i: (b, i, 0, 0)),
             jax.ShapeDtypeStruct((B, T // tm, SUBLANES, LANES), F32))]
    return pl.pallas_call(
        functools.partial(_inproj_kernel, tm=tm, d_model=D),
        grid=(B, T // tm),
        in_specs=[tok(D), per_b, per_b, const((1, D)), const((D, n_w)),
                  const((CONV_WIDTH, 3 * W_MIX)), const((tm, tm)),
                  const((1, HEAD_DIM)), const((1, HEAD_DIM)), const((1, LANES))],
        out_specs=[o[0] for o in outs],
        out_shape=[o[1] for o in outs],
        scratch_shapes=([pltpu.VMEM((1, LANES), F32)] + [pltpu.VMEM((tm + SUBLANES, W_MIX), F32)] * 3
                        + [pltpu.VMEM((D // LANES, tm, LANES), F32), pltpu.VMEM((tm, LANES), F32)]),
        compiler_params=pltpu.CompilerParams(dimension_semantics=("parallel", "arbitrary"),
                                             vmem_limit_bytes=VMEM_LIMIT),
        name="inproj",
    )(x, shift, scale, g_norm, w_perm, conv_w, tri, g_q, g_k, bf_lane)


def _gdn_masks():
    def token(n):
        within = n % GDN_CHUNK
        return n - within + SUBLANES * (within % SUBLANES) + within // SUBLANES

    row = lax.broadcasted_iota(jnp.int32, (GDN_BLOCK, GDN_BLOCK), 0)
    col_ = lax.broadcasted_iota(jnp.int32, (GDN_BLOCK, GDN_BLOCK), 1)
    r, c = token(row), token(col_)
    same = (r // GDN_CHUNK) == (c // GDN_CHUNK)
    lower = jnp.logical_and(same, r >= c)
    f_lower = jnp.where(lower, 1.0, 0.0)
    f_same = jnp.where(same, 1.0, 0.0)
    f_cum = jnp.concatenate([f_lower, f_same], axis=0).astype(BF16)
    f_upper = jnp.where(jnp.logical_and(same, r <= c), 1.0, 0.0).astype(BF16)
    eye = jnp.where(r == c, 1.0, 0.0).astype(F32)
    nm_lower = jnp.where(lower, 0.0, -MASK_BIG).astype(F32)
    nm_strict = jnp.where(jnp.logical_and(same, r > c), 0.0, -MASK_BIG).astype(F32)
    return f_cum, f_upper, eye, nm_lower, nm_strict


def _gdn_kernel(qkv_ref, slab_ref, gt_ref, acol_ref, dcol_ref, arow_ref, drow_ref,
                fcum_ref, fupper_ref, eye_ref, nml_ref, nms_ref,
                za_ref, gout_ref, o_ref, s_ref, obuf, *, tt):
    i = pl.program_id(1)
    C = GDN_CHUNK
    BLK = GDN_BLOCK
    shift = C.bit_length() - 1

    @pl.when(i == 0)
    def _():
        s_ref[...] = jnp.zeros_like(s_ref)

    f_cum = fcum_ref[...]
    f_upper = fupper_ref[...]
    eye = eye_ref[...]
    nm_lower = nml_ref[...]
    nm_strict = nms_ref[...]
    n_blk = tt // BLK
    per_blk = BLK // C
    n_fac = shift - 1
    assert per_blk == N_HEADS

    def split2(v):
        hi = v.astype(BF16)
        return hi, (v - hi.astype(F32)).astype(BF16)

    def col(slab_, lane):
        return jnp.broadcast_to(slab_[:, lane:lane + 1], (slab_.shape[0], HEAD_DIM))

    gates = {}

    def prep_gates(sb):
        rows = slice(sb * BLK, (sb + 1) * BLK)
        slab = slab_ref[rows, :]
        g_col = -jnp.exp(acol_ref[...]) * _softplus(slab + dcol_ref[...])
        g_row = -jnp.exp(arow_ref[...]) * _softplus(gt_ref[:, rows] + drow_ref[...])
        g_hi, g_lo = split2(g_col)
        cum = jnp.dot(f_cum, jnp.concatenate([g_hi, g_lo], axis=1), preferred_element_type=F32)
        cum = cum[:, :LANES] + cum[:, LANES:]
        gc_col, gl_col = cum[:BLK], cum[BLK:]
        r_hi, r_lo = split2(g_row)
        cr = jnp.dot(jnp.concatenate([r_hi, r_lo], axis=0), f_upper, preferred_element_type=F32)
        gates[sb] = dict(beta=_sigmoid(slab), gc_col=gc_col, gc_row=cr[:SUBLANES] + cr[SUBLANES:],
                         e_gc=jnp.exp(gc_col), e_rem=jnp.exp(gl_col - gc_col), e_tot=jnp.exp(gl_col))

    qk_m, rhs_uw, qg_m, kd_m, etot_m, p_m, x_m, u_m, w_m = {}, {}, {}, {}, {}, {}, {}, {}, {}

    def prep_head(sb, hd):
        pr = (sb, hd)
        gt = gates[sb]
        rows = slice(sb * BLK, (sb + 1) * BLK)
        qh = qkv_ref[rows, hd * HEAD_DIM:(hd + 1) * HEAD_DIM].astype(F32)
        kh = qkv_ref[rows, W_MIX + hd * HEAD_DIM:W_MIX + (hd + 1) * HEAD_DIM].astype(F32)
        vh = qkv_ref[rows, 2 * W_MIX + hd * HEAD_DIM:2 * W_MIX + (hd + 1) * HEAD_DIM].astype(F32)
        qn = qh * lax.rsqrt(jnp.sum(qh * qh, axis=-1, keepdims=True) + EPS) * (HEAD_DIM ** -0.5)
        kn = kh * lax.rsqrt(jnp.sum(kh * kh, axis=-1, keepdims=True) + EPS)
        bh = col(gt["beta"], _L_B + hd)
        e_gc = col(gt["e_gc"], _L_A + hd)
        kb = kn * bh
        rhs_uw[pr] = jnp.concatenate([vh * bh, kb * e_gc], axis=1).astype(BF16)
        qg_m[pr] = (qn * e_gc).astype(BF16)
        kd_m[pr] = (kn * col(gt["e_rem"], _L_A + hd)).astype(BF16)
        etot_m[pr] = col(gt["e_tot"], _L_A + hd)
        kq = jnp.dot(jnp.concatenate([kb, qn], axis=0).astype(BF16), kn.T.astype(BF16),
                     preferred_element_type=F32)
        diff = gt["gc_col"][:, _L_A + hd:_L_A + hd + 1] - gt["gc_row"][hd:hd + 1, :]
        a = kq[:BLK] * jnp.exp(diff + nm_strict)
        qk_m[pr] = (kq[BLK:] * jnp.exp(diff + nm_lower)).astype(BF16)
        ab = a.astype(BF16)
        p_m[pr] = eye - a
        x_m[pr] = jnp.dot(ab, ab, preferred_element_type=F32)

    def inv_iter(sb, j):
        for hd in range(N_HEADS):
            pr = (sb, hd)
            xb = x_m[pr].astype(BF16)
            if j + 1 < n_fac:
                px = jnp.dot(jnp.concatenate([p_m[pr].astype(BF16), xb], axis=0), xb,
                             preferred_element_type=F32)
                p_m[pr] = p_m[pr] + px[:BLK]
                x_m[pr] = px[BLK:]
            else:
                p_m[pr] = p_m[pr] + jnp.dot(p_m[pr].astype(BF16), xb, preferred_element_type=F32)

    def apply_inverse(sb):
        for hd in range(N_HEADS):
            pr = (sb, hd)
            uw = jnp.dot(p_m[pr].astype(BF16), rhs_uw[pr], preferred_element_type=F32)
            u_m[pr] = uw[:, :HEAD_DIM]
            w_m[pr] = uw[:, HEAD_DIM:].astype(BF16)

    state = [s_ref[hd] for hd in range(N_HEADS)]
    vnew_parts, qs_parts = {}, {}

    def state_chunk(sb, cb):
        rows = slice(cb * C, (cb + 1) * C)
        for hd in range(N_HEADS):
            pr = (sb, hd)
            rr = jnp.dot(jnp.concatenate([w_m[pr][rows], qg_m[pr][rows]], axis=0),
                         state[hd].astype(BF16), preferred_element_type=F32)
            vnew = u_m[pr][rows] - rr[:C]
            qs_parts.setdefault(pr, []).append(rr[C:])
            vnew_parts.setdefault(pr, []).append(vnew)
            upd = lax.dot_general(kd_m[pr][rows], vnew.astype(BF16), (((0,), (0,)), ((), ())),
                                  preferred_element_type=F32)
            state[hd] = state[hd] * etot_m[pr][cb * C:cb * C + 1, :] + upd

    def out_head(sb, hd):
        pr = (sb, hd)
        rows = slice(sb * BLK, (sb + 1) * BLK)
        sl = slice(hd * HEAD_DIM, (hd + 1) * HEAD_DIM)
        vn = jnp.concatenate(vnew_parts[pr], axis=0).astype(BF16)
        o = jnp.concatenate(qs_parts[pr], axis=0) + jnp.dot(qk_m[pr], vn, preferred_element_type=F32)
        for g in range(BLK // SUBLANES):
            ch, j = divmod(g, C // SUBLANES)
            obuf[hd, pl.ds(ch * C + j, SUBLANES, stride=SUBLANES), :] = o[g * SUBLANES:(g + 1) * SUBLANES, :]
        o = obuf[hd]
        on = o * lax.rsqrt(jnp.mean(o * o, axis=-1, keepdims=True) + EPS) * gout_ref[...]
        o_ref[rows, sl] = (on * _silu(za_ref[rows, sl].astype(F32))).astype(BF16)

    inv_share = [[] for _ in range(per_blk)]
    for j in range(n_fac):
        inv_share[max(0, j - (n_fac - per_blk))].append(j)
    for t in range(-2, n_blk + 1):
        if 0 <= t + 2 < n_blk:
            prep_gates(t + 2)
        for slot in range(per_blk):
            if 0 <= t < n_blk:
                state_chunk(t, slot)
            if 0 <= t + 1 < n_blk:
                for j in inv_share[slot]:
                    inv_iter(t + 1, j)
                if slot == per_blk - 1:
                    apply_inverse(t + 1)
            if 0 <= t + 2 < n_blk:
                prep_head(t + 2, slot)
            if 0 <= t - 1 < n_blk:
                out_head(t - 1, slot)
    for hd in range(N_HEADS):
        s_ref[hd] = state[hd]


def _gdn(qkva, slab, gates_t, a_log, dt_bias, za, g_out, tt):
    B, T, _ = qkva.shape
    tok = lambda width: pl.BlockSpec((None, tt, width), lambda b, i: (b, i, 0))
    const = lambda shape: pl.BlockSpec(shape, lambda b, i: (0,) * len(shape), pipeline_mode=pl.Buffered(1))
    lane_vec = lambda v: jnp.zeros((1, LANES), F32).at[0, _L_A:_L_A + N_HEADS].set(v)
    sub_vec = lambda v: jnp.zeros((SUBLANES, 1), F32).at[0:N_HEADS, 0].set(v)
    masks = _gdn_masks()
    return pl.pallas_call(
        functools.partial(_gdn_kernel, tt=tt),
        grid=(B, T // tt),
        in_specs=[tok(3 * W_MIX), tok(LANES),
                  pl.BlockSpec((None, SUBLANES, tt), lambda b, i: (b, 0, i)),
                  const((1, LANES)), const((1, LANES)), const((SUBLANES, 1)), const((SUBLANES, 1))]
                 + [const(m.shape) for m in masks]
                 + [tok(W_MIX), const((1, HEAD_DIM))],
        out_specs=tok(W_MIX),
        out_shape=jax.ShapeDtypeStruct((B, T, W_MIX), BF16),
        scratch_shapes=[pltpu.VMEM((N_HEADS, HEAD_DIM, HEAD_DIM), F32),
                        pltpu.VMEM((N_HEADS, GDN_BLOCK, HEAD_DIM), F32)],
        compiler_params=pltpu.CompilerParams(dimension_semantics=("parallel", "arbitrary"),
                                             vmem_limit_bytes=VMEM_LIMIT),
        name="gdn",
    )(qkva, slab, gates_t, lane_vec(a_log), lane_vec(dt_bias), sub_vec(a_log), sub_vec(dt_bias),
      *masks, za, g_out.reshape(1, HEAD_DIM))


def _fox_kernel(fp_ref, qt_ref, k_ref, vt_ref, zf_ref, o_ref, m_ref, acc_ref, sa_ref, sb_ref, *, tq, tk, n_q):
    first_pair_base = (pl.program_id(0) * N_HEADS + pl.program_id(1)) * n_q
    ones_rows = jnp.ones((FOX_SUM_ROWS, tk), BF16)
    halves = (slice(0, tk), slice(tk, 2 * tk))

    def scores_into(buf, kb, qi, which=(0, 1)):
        k = k_ref[pl.ds(pl.multiple_of(kb * tk, tk), tk), :]
        for h in which:
            buf[h] = jnp.dot(k, qt_ref[2 * qi + h], preferred_element_type=F32)

    def consume(buf, kb, modes):
        vt = jnp.concatenate([vt_ref[kb], ones_rows], axis=0)
        live = [h for h in (0, 1) if modes[h] != "skip"]
        pts, alphas = {}, {}
        for h in live:
            st = buf[h]
            if modes[h] == "diag":
                krow = lax.broadcasted_iota(jnp.int32, (tk, tk), 0)
                qcol = lax.broadcasted_iota(jnp.int32, (tk, tk), 1)
                st = jnp.where(qcol >= krow, st, NEG_INF)
            m_prev = m_ref[:, halves[h]]
            m_new = jnp.maximum(m_prev, jnp.max(st, axis=0, keepdims=True))
            alphas[h] = jnp.exp2(m_prev - m_new)
            pts[h] = jnp.exp2(st - m_new).astype(BF16)
            m_ref[:, halves[h]] = m_new
        for h in live:
            acc_ref[:, halves[h]] = (alphas[h] * acc_ref[:, halves[h]]
                                     + jnp.dot(vt, pts[h], preferred_element_type=F32))

    full = ("full", "full")
    p0 = fp_ref[first_pair_base]
    scores_into(sa_ref, 2 * p0, 0)
    for qi in range(n_q):
        m_ref[...] = jnp.full_like(m_ref, NEG_INF)
        acc_ref[...] = jnp.zeros_like(acc_ref)

        def pair(p, carry, qi=qi):
            kb = 2 * p
            scores_into(sb_ref, kb + 1, qi)
            consume(sa_ref, kb, full)
            scores_into(sa_ref, kb + 2, qi)
            consume(sb_ref, kb + 1, full)
            return carry

        lax.fori_loop(p0, qi, pair, 0)
        scores_into(sb_ref, 2 * qi + 1, qi, which=(1,))
        consume(sa_ref, 2 * qi, ("diag", "full"))
        if qi + 1 < n_q:
            p0 = fp_ref[first_pair_base + qi + 1]
            scores_into(sa_ref, 2 * p0, qi + 1)
        consume(sb_ref, 2 * qi + 1, ("skip", "diag"))

        rows = slice(qi * tq, (qi + 1) * tq)
        ot = acc_ref[0:HEAD_DIM, :] * (1.0 / acc_ref[HEAD_DIM:HEAD_DIM + 1, :])
        o_ref[rows, :] = (ot.T * _silu(zf_ref[rows, :].astype(F32))).astype(BF16)


def _fox_first_pair(stats, tk, n_q):
    c_first = stats[:, :, 0, _L_F:_L_F + N_HEADS]
    c_last = stats[:, :, 1, _L_F:_L_F + N_HEADS]
    qk_bound = jnp.sqrt(jnp.max(stats[:, :, 2, :N_HEADS], axis=1) * jnp.max(stats[:, :, 3, :N_HEADS], axis=1))
    n_kb = stats.shape[1]
    upper = (2.0 * qk_bound[:, None, None, :] + c_first[:, 0::2, None, :][:, :n_q]
             - c_last[:, None, :, :])
    kb = jnp.arange(n_kb, dtype=jnp.int32)[None, None, :, None]
    diag = 2 * jnp.arange(n_q, dtype=jnp.int32)[None, :, None, None]
    needed = jnp.logical_or(upper >= -FOX_SKIP_LOG2, kb >= diag)
    first_block = jnp.min(jnp.where(needed, kb, n_kb), axis=2)
    return jnp.right_shift(first_block, 1).transpose(0, 2, 1).reshape(-1)


def _fox(qf_aug, kf_aug, vft, zf, stats, tk):
    B, _, T, _ = zf.shape
    tq = 2 * tk
    assert T % tq == 0
    n_q = T // tq
    first_pair = _fox_first_pair(stats, tk, n_q)
    grid_spec = pltpu.PrefetchScalarGridSpec(
        num_scalar_prefetch=1,
        grid=(B, N_HEADS),
        in_specs=[pl.BlockSpec((None, None, T // tk, 2 * HEAD_DIM, tk), lambda b, h, fp: (b, h, 0, 0, 0)),
                  pl.BlockSpec((None, None, T, 2 * HEAD_DIM), lambda b, h, fp: (b, h, 0, 0)),
                  pl.BlockSpec((None, None, T // tk, HEAD_DIM, tk), lambda b, h, fp: (b, h, 0, 0, 0)),
                  pl.BlockSpec((None, None, T, HEAD_DIM), lambda b, h, fp: (b, h, 0, 0))],
        out_specs=pl.BlockSpec((None, None, T, HEAD_DIM), lambda b, h, fp: (b, h, 0, 0)),
        scratch_shapes=[pltpu.VMEM((1, tq), F32), pltpu.VMEM((HEAD_DIM + FOX_SUM_ROWS, tq), F32),
                        pltpu.VMEM((2, tk, tk), F32), pltpu.VMEM((2, tk, tk), F32)])
    return pl.pallas_call(
        functools.partial(_fox_kernel, tq=tq, tk=tk, n_q=n_q),
        grid_spec=grid_spec,
        out_shape=jax.ShapeDtypeStruct((B, N_HEADS, T, HEAD_DIM), BF16),
        compiler_params=pltpu.CompilerParams(dimension_semantics=("parallel", "parallel"),
                                             vmem_limit_bytes=VMEM_LIMIT),
        name="fox",
    )(first_pair, qf_aug, kf_aug, vft, zf)


def _outproj_kernel(oa_ref, of_ref, ga_ref, gf_ref, x_ref, gate_ref, wa_ref, wf_ref, wo_ref, o_ref):
    ya = jnp.dot(oa_ref[...], wa_ref[...], preferred_element_type=F32)
    of = jnp.concatenate([of_ref[hd] for hd in range(N_HEADS)], axis=1)
    yf = jnp.dot(of, wf_ref[...], preferred_element_type=F32)
    merged = _sigmoid(ga_ref[...].astype(F32)) * ya + _sigmoid(gf_ref[...].astype(F32)) * yf
    res = jnp.dot(merged.astype(BF16), wo_ref[...], preferred_element_type=F32)
    o_ref[...] = x_ref[...] + gate_ref[...] * res


def _outproj(oa, of, ga, gf, x, gate, wa, wf, wo, tm):
    B, T, D = x.shape
    tok = lambda width: pl.BlockSpec((None, tm, width), lambda b, i: (b, i, 0))
    const = lambda shape: pl.BlockSpec(shape, lambda b, i: (0,) * len(shape), pipeline_mode=pl.Buffered(1))
    return pl.pallas_call(
        _outproj_kernel,
        grid=(B, T // tm),
        in_specs=[tok(W_MIX), pl.BlockSpec((None, N_HEADS, tm, HEAD_DIM), lambda b, i: (b, 0, i, 0)),
                  tok(D), tok(D), tok(D),
                  pl.BlockSpec((None, 1, D), lambda b, i: (b, 0, 0)),
                  const((W_MIX, D)), const((W_MIX, D)), const((D, D))],
        out_specs=tok(D),
        out_shape=jax.ShapeDtypeStruct((B, T, D), F32),
        compiler_params=pltpu.CompilerParams(dimension_semantics=("parallel", "parallel"),
                                             vmem_limit_bytes=VMEM_LIMIT),
        name="outproj",
    )(oa, of, ga, gf, x, gate, wa, wf, wo)


def _permute_w_in(w_in, d_model):
    o = 0
    cols = {}
    for name, width in (("qkva", 3 * W_MIX), ("a", N_HEADS), ("b", N_HEADS), ("za", W_MIX),
                        ("qf", W_MIX), ("kf", W_MIX), ("vf", W_MIX), ("f", N_HEADS), ("zf", W_MIX),
                        ("ga", d_model), ("gf", d_model)):
        cols[name] = w_in[:, o:o + width]
        o += width
    gates = jnp.concatenate([cols["a"], cols["b"], cols["f"],
                             jnp.zeros((w_in.shape[0], LANES - 3 * N_HEADS), w_in.dtype)], axis=1)
    return jnp.concatenate([cols["qkva"], cols["za"], cols["qf"], cols["kf"], cols["vf"], cols["zf"],
                            cols["ga"], cols["gf"], gates], axis=1).astype(BF16)


def kernel(x, c, w_ada, b_ada, g_norm, w_in, conv_w, A_log, dt_bias, g_gdn_out,
           g_q_fox, g_k_fox, b_f, w_o_gdn, w_o_fox, w_out):
    B, T, D = x.shape
    tm = tk = min(512, T)
    tt = min(4 * GDN_BLOCK, T)

    mod = _adaln(c, w_ada, b_ada)
    shift, scale, gate = (mod[:, j * D:(j + 1) * D].reshape(B, 1, D) for j in range(3))

    w_perm = _permute_w_in(w_in, D)
    bf_lane = jnp.zeros((1, LANES), F32).at[0, _L_F:_L_F + N_HEADS].set(b_f)
    (qkva, za, qf_aug, kf_aug, vft, zf, ga, gf, slab, stats) = _inproj(
        x, shift, scale, g_norm.reshape(1, D), w_perm, conv_w,
        g_q_fox.reshape(1, HEAD_DIM), g_k_fox.reshape(1, HEAD_DIM), bf_lane, tm)

    gates_t = jnp.transpose(slab[:, :, :SUBLANES], (0, 2, 1))
    oa = _gdn(qkva, slab, gates_t, A_log, dt_bias, za, g_gdn_out, tt)
    of = _fox(qf_aug, kf_aug, vft, zf, stats, tk)
    return _outproj(oa, of, ga, gf, x, gate, w_o_gdn.astype(BF16), w_o_fox.astype(BF16),
                    w_out.astype(BF16), min(2 * tm, T))
```
